```python
import math
import jax, jax.numpy as jnp
from jax import lax
import numpy as np

D_MODEL = 1024
BATCH = 8
SEQ = 2048
DEPTH = 1
DEC_BATCH = 128
DEC_SEQ = 8
PAST_LEN = 16384
PAGE_SIZE = 128

SSD_HEADS = 16
SSD_HEADDIM = 64
SSD_INNER = SSD_HEADS * SSD_HEADDIM
SSD_GROUPS = 2
SSD_STATE = 64
CONV_W = 4
CONV_DIM = SSD_INNER + 2 * SSD_GROUPS * SSD_STATE
GLA_HEADS = 4
GLA_DK = 128
GLA_DV = 256
GLA_KEY = GLA_HEADS * GLA_DK
GLA_VAL = GLA_HEADS * GLA_DV
GLA_GATE_RANK = 16
GLA_GATE_NORM = 16.0
CHUNK = 64
IN_SIZES = (SSD_INNER, CONV_DIM, SSD_HEADS, GLA_KEY, GLA_KEY, GLA_VAL, GLA_GATE_RANK, GLA_VAL, D_MODEL, D_MODEL)
IN_DIM = sum(IN_SIZES)
PEER_HEADS = 8
N_KEYS = 128
N_EXPERTS = N_KEYS * N_KEYS
PEER_KEY = 256
PEER_HALF = PEER_KEY // 2
PEER_TOPK = 16
PEER_BLOCK = 256
EPS = 1e-6

kernel_name = 'hybrid_ssd_gla_peer_step'


def rmsnorm(x, w):
    xf = x.astype(jnp.float32)
    y = xf * lax.rsqrt(jnp.mean(xf * xf, axis=-1, keepdims=True) + EPS)
    return (y * w.astype(jnp.float32)).astype(x.dtype)


def causal_conv(xbc, prev, w, b):
    L = xbc.shape[1]
    xp = jnp.concatenate([prev.astype(xbc.dtype), xbc], axis=1)
    y = b
    for k in range(CONV_W):
        y = y + w[k] * xp[:, k:k + L]
    return jax.nn.silu(y), xp[:, L:]


def ssd_chunked(x, dt, A, B, C, h0):
    f32 = jnp.float32
    bsz, L, H, P = x.shape
    Q = math.gcd(L, CHUNK)
    nc = L // Q
    rep = H // SSD_GROUPS
    x = x.astype(f32)
    B = jnp.repeat(B.astype(f32), rep, axis=2)
    C = jnp.repeat(C.astype(f32), rep, axis=2)
    a = dt * A
    xdt = x * dt[..., None]
    def chunks(t):
        return t.reshape((bsz, nc, Q) + t.shape[2:])
    a, xdt, B, C = chunks(a), chunks(xdt), chunks(B), chunks(C)
    acum = jnp.cumsum(a, axis=2)
    causal = jnp.tril(jnp.ones((Q, Q), dtype=bool))[None, None, :, :, None]
    seg = acum[:, :, :, None, :] - acum[:, :, None, :, :]
    decay = jnp.where(causal, jnp.exp(jnp.where(causal, seg, 0.0)), 0.0)
    cb = jnp.einsum('bcihn,bcjhn->bcijh', C, B)
    y_diag = jnp.einsum('bcijh,bcjhp->bcihp', cb * decay, xdt)
    to_end = jnp.exp(acum[:, :, -1:, :] - acum)
    s_chunk = jnp.einsum('bcjhn,bcjhp->bchpn', B * to_end[..., None], xdt)
    chunk_decay = jnp.exp(acum[:, :, -1, :])
    def step(h, inp):
        s_c, d_c = inp
        return d_c[:, :, None, None] * h + s_c, h
    h_last, h_in = lax.scan(step, h0.astype(f32), (jnp.moveaxis(s_chunk, 1, 0), jnp.moveaxis(chunk_decay, 1, 0)))
    h_in = jnp.moveaxis(h_in, 0, 1)
    y_off = jnp.einsum('bcihn,bchpn->bcihp', C * jnp.exp(acum)[..., None], h_in)
    return (y_diag + y_off).reshape(bsz, L, H, P), h_last


def gla_chunked(q, k, v, g, s0):
    f32 = jnp.float32
    bsz, L, H, K = q.shape
    V = v.shape[-1]
    Q = math.gcd(L, CHUNK)
    nc = L // Q
    q = q.astype(f32) * (K ** -0.5)
    k, v, g = k.astype(f32), v.astype(f32), g.astype(f32)
    def chunks(t):
        return t.reshape((bsz, nc, Q) + t.shape[2:])
    q, k, v, g = chunks(q), chunks(k), chunks(v), chunks(g)
    G = jnp.cumsum(g, axis=2)
    qe = q * jnp.exp(G)
    ke = k * jnp.exp(-G)
    causal = jnp.tril(jnp.ones((Q, Q), dtype=bool))
    att = jnp.where(causal, jnp.einsum('bcihk,bcjhk->bchij', qe, ke), 0.0)
    o_intra = jnp.einsum('bchij,bcjhv->bcihv', att, v)
    kd = k * jnp.exp(G[:, :, -1:] - G)
    s_chunk = jnp.einsum('bcjhk,bcjhv->bchkv', kd, v)
    chunk_decay = jnp.exp(G[:, :, -1])
    def step(s, inp):
        s_c, d_c = inp
        return d_c[..., None] * s + s_c, s
    s_last, s_in = lax.scan(step, s0.astype(f32), (jnp.moveaxis(s_chunk, 1, 0), jnp.moveaxis(chunk_decay, 1, 0)))
    s_in = jnp.moveaxis(s_in, 0, 1)
    o_inter = jnp.einsum('bcihk,bchkv->bcihv', qe, s_in)
    return (o_intra + o_inter).reshape(bsz, L, H, V), s_last


def hybrid_mixer(h, conv_prev, ssd_h0, gla_s0, w_in, conv_w, conv_b, dt_bias, a_log, d_skip,
                 ssd_norm_w, w_gla_gate2, b_gla_gate, gla_norm_w, w_ssd_out, w_gla_out, w_out):
    bsz, L, _ = h.shape
    proj = h @ w_in
    splits = np.cumsum(IN_SIZES)[:-1].tolist()
    z, xbc, dt, q, k, v, glr, r, gate_ssd, gate_gla = jnp.split(proj, splits, axis=-1)
    xbc, conv_new = causal_conv(xbc, conv_prev, conv_w, conv_b)
    xs, Bm, Cm = jnp.split(xbc, [SSD_INNER, SSD_INNER + SSD_GROUPS * SSD_STATE], axis=-1)
    dt = jax.nn.softplus(dt.astype(jnp.float32) + dt_bias.astype(jnp.float32))
    A = -jnp.exp(a_log.astype(jnp.float32))
    xs4 = xs.reshape(bsz, L, SSD_HEADS, SSD_HEADDIM)
    ys, ssd_new = ssd_chunked(xs4, dt, A,
                              Bm.reshape(bsz, L, SSD_GROUPS, SSD_STATE),
                              Cm.reshape(bsz, L, SSD_GROUPS, SSD_STATE), ssd_h0)
    ys = ys + d_skip.astype(jnp.float32)[:, None] * xs4.astype(jnp.float32)
    ys = ys.reshape(bsz, L, SSD_INNER).astype(h.dtype)
    ys = rmsnorm(ys * jax.nn.silu(z), ssd_norm_w)
    g = jax.nn.log_sigmoid((glr @ w_gla_gate2 + b_gla_gate).astype(jnp.float32)) / GLA_GATE_NORM
    o, gla_new = gla_chunked(q.reshape(bsz, L, GLA_HEADS, GLA_DK),
                             k.reshape(bsz, L, GLA_HEADS, GLA_DK),
                             v.reshape(bsz, L, GLA_HEADS, GLA_DV),
                             g.reshape(bsz, L, GLA_HEADS, GLA_DK), gla_s0)
    o = rmsnorm(o.astype(h.dtype), gla_norm_w).reshape(bsz, L, GLA_VAL)
    o = o * jax.nn.silu(r)
    mix = jax.nn.sigmoid(gate_ssd) * (ys @ w_ssd_out) + jax.nn.sigmoid(gate_gla) * (o @ w_gla_out)
    return mix @ w_out, conv_new, ssd_new, gla_new


def peer_ffn(h, w_query, sub_keys, expert_u, expert_v):
    bsz, L, D = h.shape
    n = bsz * L
    hf = h.reshape(n, D)
    q = (hf @ w_query).reshape(n, PEER_HEADS, 2, PEER_HALF)
    s = jnp.einsum('nhpd,hpkd->nhpk', q, sub_keys).astype(jnp.float32)
    v1, i1 = lax.top_k(s[:, :, 0], PEER_TOPK)
    v2, i2 = lax.top_k(s[:, :, 1], PEER_TOPK)
    cand = (v1[..., :, None] + v2[..., None, :]).reshape(n, PEER_HEADS, PEER_TOPK * PEER_TOPK)
    top, pos = lax.top_k(cand, PEER_TOPK)
    e1 = jnp.take_along_axis(i1, pos // PEER_TOPK, axis=-1)
    e2 = jnp.take_along_axis(i2, pos % PEER_TOPK, axis=-1)
    idx = e1 * N_KEYS + e2
    gates = jax.nn.softmax(top, axis=-1)
    blk = math.gcd(n, PEER_BLOCK)
    nb = n // blk
    def block(args):
        xb, ib, gb = args
        u = expert_u[ib]
        act = jax.nn.gelu(jnp.einsum('thkd,td->thk', u, xb).astype(jnp.float32), approximate=False)
        wgt = (gb * act).astype(xb.dtype)
        return jnp.einsum('thk,thkd->td', wgt, expert_v[ib])
    out = lax.map(block, (hf.reshape(nb, blk, D),
                          idx.reshape(nb, blk, PEER_HEADS, PEER_TOPK),
                          gates.reshape(nb, blk, PEER_HEADS, PEER_TOPK)))
    return out.reshape(bsz, L, D).astype(h.dtype)


def run_trunk(x, conv_state, ssd_state, gla_state, norm_mix_w, w_in, conv_w, conv_b, dt_bias, a_log,
              d_skip, ssd_norm_w, w_gla_gate2, b_gla_gate, gla_norm_w, w_ssd_out, w_gla_out, w_out,
              norm_ffn_w, w_query, sub_keys, expert_u, expert_v, final_norm_w):
    convs, ssds, glas = [], [], []
    for l in range(DEPTH):
        mix, c_new, s_new, g_new = hybrid_mixer(
            rmsnorm(x, norm_mix_w[l]), conv_state[l], ssd_state[l], gla_state[l],
            w_in[l], conv_w[l], conv_b[l], dt_bias[l], a_log[l], d_skip[l], ssd_norm_w[l],
            w_gla_gate2[l], b_gla_gate[l], gla_norm_w[l], w_ssd_out[l], w_gla_out[l], w_out[l])
        x = x + mix
        x = x + peer_ffn(rmsnorm(x, norm_ffn_w[l]), w_query[l], sub_keys[l], expert_u[l], expert_v[l])
        convs.append(c_new)
        ssds.append(s_new)
        glas.append(g_new)
    return rmsnorm(x, final_norm_w), jnp.stack(convs), jnp.stack(ssds), jnp.stack(glas)


def setup_inputs(seed: int = 0) -> dict:
    key = jax.random.key(seed)
    ks = iter(jax.random.split(key, 40))
    def nrm(shape, scale):
        return jax.random.normal(next(ks), shape, jnp.float32) * scale
    def gain(shape):
        return 1.0 + nrm(shape, 0.01)
    dt0 = jnp.exp(jax.random.uniform(next(ks), (DEPTH, SSD_HEADS), jnp.float32, math.log(1e-3), math.log(1e-1)))
    a0 = jax.random.uniform(next(ks), (DEPTH, SSD_HEADS), jnp.float32, 1.0, 16.0)
    return {
        'x_prompt': nrm((BATCH, SEQ, D_MODEL), 1.0),
        'x_sample': nrm((DEC_BATCH, DEC_SEQ, D_MODEL), 1.0),
        'state_conv': nrm((DEPTH, DEC_BATCH, CONV_W - 1, CONV_DIM), 1.0),
        'state_ssd': nrm((DEPTH, DEC_BATCH, SSD_HEADS, SSD_HEADDIM, SSD_STATE), 0.1),
        'state_gla': nrm((DEPTH, DEC_BATCH, GLA_HEADS, GLA_DK, GLA_DV), 0.5),
        'norm_mix_w': gain((DEPTH, D_MODEL)),
        'w_in': nrm((DEPTH, D_MODEL, IN_DIM), D_MODEL ** -0.5),
        'conv_w': nrm((DEPTH, CONV_W, CONV_DIM), CONV_W ** -0.5),
        'conv_b': nrm((DEPTH, CONV_DIM), 0.01),
        'dt_bias': dt0 + jnp.log(-jnp.expm1(-dt0)),
        'a_log': jnp.log(a0),
        'd_skip': gain((DEPTH, SSD_HEADS)),
        'ssd_norm_w': gain((DEPTH, SSD_INNER)),
        'w_gla_gate2': nrm((DEPTH, GLA_GATE_RANK, GLA_KEY), GLA_GATE_RANK ** -0.5),
        'b_gla_gate': nrm((DEPTH, GLA_KEY), 0.01),
        'gla_norm_w': gain((DEPTH, GLA_DV)),
        'w_ssd_out': nrm((DEPTH, SSD_INNER, D_MODEL), SSD_INNER ** -0.5),
        'w_gla_out': nrm((DEPTH, GLA_VAL, D_MODEL), GLA_VAL ** -0.5),
        'w_out': nrm((DEPTH, D_MODEL, D_MODEL), D_MODEL ** -0.5),
        'norm_ffn_w': gain((DEPTH, D_MODEL)),
        'w_query': nrm((DEPTH, D_MODEL, PEER_HEADS * PEER_KEY), D_MODEL ** -0.5),
        'sub_keys': nrm((DEPTH, PEER_HEADS, 2, N_KEYS, PEER_HALF), PEER_HALF ** -0.5),
        'expert_u': nrm((DEPTH, N_EXPERTS, D_MODEL), D_MODEL ** -0.5),
        'expert_v': nrm((DEPTH, N_EXPERTS, D_MODEL), 0.5),
        'final_norm_w': gain((D_MODEL,)),
    }


def reference(x_prompt, x_sample, state_conv, state_ssd, state_gla, norm_mix_w, w_in, conv_w, conv_b,
              dt_bias, a_log, d_skip, ssd_norm_w, w_gla_gate2, b_gla_gate, gla_norm_w, w_ssd_out,
              w_gla_out, w_out, norm_ffn_w, w_query, sub_keys, expert_u, expert_v, final_norm_w):
    bp = x_prompt.shape[0]
    conv0 = jnp.zeros((DEPTH, bp, CONV_W - 1, CONV_DIM), x_prompt.dtype)
    ssd0 = jnp.zeros((DEPTH, bp, SSD_HEADS, SSD_HEADDIM, SSD_STATE), jnp.float32)
    gla0 = jnp.zeros((DEPTH, bp, GLA_HEADS, GLA_DK, GLA_DV), jnp.float32)
    y_prompt, conv_p, ssd_p, gla_p = run_trunk(
        x_prompt, conv0, ssd0, gla0, norm_mix_w, w_in, conv_w, conv_b, dt_bias, a_log, d_skip,
        ssd_norm_w, w_gla_gate2, b_gla_gate, gla_norm_w, w_ssd_out, w_gla_out, w_out, norm_ffn_w,
        w_query, sub_keys, expert_u, expert_v, final_norm_w)
    y_sample, conv_s, ssd_s, gla_s = run_trunk(
        x_sample, state_conv, state_ssd, state_gla, norm_mix_w, w_in, conv_w, conv_b, dt_bias, a_log,
        d_skip, ssd_norm_w, w_gla_gate2, b_gla_gate, gla_norm_w, w_ssd_out, w_gla_out, w_out,
        norm_ffn_w, w_query, sub_keys, expert_u, expert_v, final_norm_w)
    return (y_prompt, y_sample, conv_p, ssd_p, gla_p, conv_s, ssd_s, gla_s)
```

```python
import functools
import math

import jax
import jax.numpy as jnp
from jax import lax
from jax.experimental import pallas as pl
from jax.experimental.pallas import tpu as pltpu

F32 = jnp.float32
BF16 = jnp.bfloat16
HIGHEST = lax.Precision.HIGHEST

D_MODEL = 1024
SSD_HEADS = 16
SSD_HEADDIM = 64
SSD_INNER = 1024
SSD_GROUPS = 2
SSD_STATE = 64
CONV_W = 4
CONV_DIM = 1280
GLA_HEADS = 4
GLA_DK = 128
GLA_DV = 256
GLA_KEY = 512
GLA_VAL = 1024
GLA_GATE_RANK = 16
GLA_GATE_NORM = 16.0
IN_SIZES = (SSD_INNER, CONV_DIM, SSD_HEADS, GLA_KEY, GLA_KEY, GLA_VAL, GLA_GATE_RANK, GLA_VAL, D_MODEL, D_MODEL)
PEER_HEADS = 8
N_KEYS = 128
PEER_HALF = 128
PEER_TOPK = 16
EPS = 1e-6

LANES = 128
ROWS = 128
VMEM_LIMIT = 56 * 1024 * 1024

COL_Z, COL_V, COL_R, COL_GS, COL_GG = 0, 1024, 2048, 3072, 4096
COL_Q, COL_K, COL_XS, COL_BC, COL_DTG = 5120, 5632, 6144, 7168, 7424
PROJ_COLS = 7680

NT_DIMS = (((1,), (1,)), ((), ()))


def _cparams(sem):
    return pltpu.CompilerParams(dimension_semantics=sem, vmem_limit_bytes=VMEM_LIMIT)


def _dot(a, b):
    return jnp.dot(a, b, preferred_element_type=F32)


def _dot_hi(a, b):
    return jnp.dot(a, b, preferred_element_type=F32, precision=HIGHEST)


def _dot_nt(a, b):
    return lax.dot_general(a, b, NT_DIMS, preferred_element_type=F32)


def _silu(x):
    return x * jax.nn.sigmoid(x)


def _proj_body(x_ref, nw_ref, w_ref, o_ref, h_scr):
    @pl.when(pl.program_id(1) == 0)
    def _():
        x = x_ref[...]
        ms = jnp.mean(x * x, axis=-1, keepdims=True)
        h_scr[...] = (x * lax.rsqrt(ms + EPS) * nw_ref[...]).astype(BF16)

    o_ref[...] = _dot(h_scr[...], w_ref[...])


def _proj_call(x2d, norm_w, w_perm, tm, tn):
    n = x2d.shape[0]
    return pl.pallas_call(
        _proj_body,
        grid=(n // tm, PROJ_COLS // tn),
        in_specs=[
            pl.BlockSpec((tm, D_MODEL), lambda i, j: (i, 0)),
            pl.BlockSpec((1, D_MODEL), lambda i, j: (0, 0)),
            pl.BlockSpec((D_MODEL, tn), lambda i, j: (0, j)),
        ],
        out_specs=pl.BlockSpec((tm, tn), lambda i, j: (i, j)),
        out_shape=jax.ShapeDtypeStruct((n, PROJ_COLS), F32),
        scratch_shapes=[pltpu.VMEM((tm, D_MODEL), BF16)],
        compiler_params=_cparams(("parallel", "arbitrary")),
        name="proj",
    )(x2d, norm_w, w_perm)


def _block_masks(rows, q):
    ri = lax.broadcasted_iota(jnp.int32, (rows, rows), 0)
    ci = lax.broadcasted_iota(jnp.int32, (rows, rows), 1)
    shift = int(math.log2(q))
    same = (ri >> shift) == (ci >> shift)
    causal = same & (ci <= ri)
    return same, causal


def _row_select(rows, width, r0, q):
    rcol = lax.broadcasted_iota(jnp.int32, (rows, 1), 0)
    rowm = (rcol >= r0) & (rcol < r0 + q)
    rfull = lax.broadcasted_iota(jnp.int32, (rows, width), 0)
    sel = jnp.where(rfull == r0 + (q - 1), 1.0, 0.0).astype(F32)
    return rowm, sel


def _ssd_body(xs_ref, bc_ref, dtg_ref, z_ref, cprev_ref, h0_ref, cw_ref, cb_ref, dtb_ref, alog_ref,
              dsk_ref, nw_ref, eexp_ref, ys_ref, cnew_ref, hout_ref,
              xext, hst, c_scr, b_scr, xwt_scr, cdt_scr, yoff_scr, *, bg, q):
    rows = bg * q
    c = pl.program_id(1)
    ngroup = SSD_HEADS // SSD_GROUPS
    gw = ngroup * SSD_HEADDIM

    @pl.when(c == 0)
    def _():
        xext[:, 5:8, :] = cprev_ref[...]
        hst[...] = h0_ref[...]

    xext[:, 8:8 + q, 0:SSD_INNER] = xs_ref[...]
    xext[:, 8:8 + q, SSD_INNER:CONV_DIM] = bc_ref[...]
    acc = cb_ref[...][None]
    for k in range(CONV_W):
        acc = acc + cw_ref[k:k + 1, :][None] * xext[:, 5 + k:5 + k + q, :]
    xc = _silu(acc).reshape(rows, CONV_DIM)
    tail = xext[:, q + 5:q + 8, :]
    cnew_ref[...] = tail
    xext[:, 5:8, :] = tail

    xs = xc[:, :SSD_INNER]
    bm = xc[:, SSD_INNER:SSD_INNER + LANES]
    cm = xc[:, SSD_INNER + LANES:CONV_DIM]

    lane = lax.broadcasted_iota(jnp.int32, (1, LANES), 1)
    hmask = lane < SSD_HEADS
    dt = jnp.where(hmask, jax.nn.softplus(dtg_ref[...].reshape(rows, LANES) + dtb_ref[...]), 0.0)
    a = dt * jnp.where(hmask, -jnp.exp(alog_ref[...]), 0.0)

    same, causal = _block_masks(rows, q)
    acum = _dot_hi(causal.astype(F32), a)
    alast = _dot_hi(same.astype(F32), a)

    eexp = eexp_ref[...]
    acum_x = _dot_hi(acum, eexp)
    alast_x = _dot_hi(alast, eexp)
    xdt = xs * _dot_hi(dt, eexp)
    xw = xdt * jnp.exp(alast_x - acum_x)

    cbs = []
    for g in range(SSD_GROUPS):
        cb = _dot_nt(cm[:, g * SSD_STATE:(g + 1) * SSD_STATE].astype(BF16),
                     bm[:, g * SSD_STATE:(g + 1) * SSD_STATE].astype(BF16))
        cbs.append(jnp.where(causal, cb, 0.0))
    ydiag = []
    for h in range(SSD_HEADS):
        col = jnp.broadcast_to(acum[:, h:h + 1], (rows, rows))
        dec = jnp.exp(jnp.where(causal, col - col.T, 0.0))
        w = (cbs[h // ngroup] * dec).astype(BF16)
        ydiag.append(_dot(w, xdt[:, h * SSD_HEADDIM:(h + 1) * SSD_HEADDIM].astype(BF16)))
    y = jnp.concatenate(ydiag, axis=1)

    c_scr[...] = cm
    b_scr[...] = bm
    cd_x = jnp.exp(alast_x)
    for g in range(SSD_GROUPS):
        xwt_scr[g] = xw[:, g * gw:(g + 1) * gw].T.astype(BF16)
        cdt_scr[g] = cd_x[:, g * gw:(g + 1) * gw].T

    def seq_step(b, carry):
        r0 = pl.multiple_of(b * q, q)
        rowm, sel = _row_select(rows, SSD_STATE, r0, q)
        for g in range(SSD_GROUPS):
            hg = hst[b, g * ngroup:(g + 1) * ngroup].reshape(gw, SSD_STATE)
            cg = c_scr[pl.ds(r0, q), g * SSD_STATE:(g + 1) * SSD_STATE]
            yoff_scr[pl.ds(r0, q), g * gw:(g + 1) * gw] = _dot_nt(cg.astype(BF16), hg.astype(BF16))
            bmask = jnp.where(rowm, b_scr[:, g * SSD_STATE:(g + 1) * SSD_STATE], 0.0).astype(BF16)
            s_new = _dot(xwt_scr[g], bmask)
            cd = _dot_hi(cdt_scr[g], sel)
            hst[b, g * ngroup:(g + 1) * ngroup] = (cd * hg + s_new).reshape(ngroup, SSD_HEADDIM, SSD_STATE)
        return carry

    lax.fori_loop(0, bg, seq_step, 0)

    y = y + yoff_scr[...] * jnp.exp(acum_x) + dsk_ref[...] * xs
    yz = y * _silu(z_ref[...].reshape(rows, SSD_INNER))
    ms = jnp.mean(yz * yz, axis=-1, keepdims=True)
    ys_ref[...] = (yz * lax.rsqrt(ms + EPS) * nw_ref[...]).astype(BF16).reshape(bg, q, SSD_INNER)

    @pl.when(c == pl.num_programs(1) - 1)
    def _():
        hout_ref[...] = hst[...]


def _ssd_call(p3, conv_prev, h0, conv_w, conv_b, dtb, alog, dsk_x, norm_w, eexp, bg, q):
    nb, length, _ = p3.shape
    full = lambda shape: pl.BlockSpec(shape, lambda i, c: (0,) * len(shape))
    rows = bg * q
    gw = SSD_INNER // SSD_GROUPS
    return pl.pallas_call(
        functools.partial(_ssd_body, bg=bg, q=q),
        grid=(nb // bg, length // q),
        in_specs=[
            pl.BlockSpec((bg, q, SSD_INNER), lambda i, c: (i, c, COL_XS // SSD_INNER)),
            pl.BlockSpec((bg, q, 2 * LANES), lambda i, c: (i, c, COL_BC // (2 * LANES))),
            pl.BlockSpec((bg, q, LANES), lambda i, c: (i, c, COL_DTG // LANES)),
            pl.BlockSpec((bg, q, SSD_INNER), lambda i, c: (i, c, COL_Z // SSD_INNER)),
            pl.BlockSpec((bg, CONV_W - 1, CONV_DIM), lambda i, c: (i, 0, 0)),
            pl.BlockSpec((bg, SSD_HEADS, SSD_HEADDIM, SSD_STATE), lambda i, c: (i, 0, 0, 0)),
            full((CONV_W, CONV_DIM)), full((1, CONV_DIM)), full((1, LANES)), full((1, LANES)),
            full((1, SSD_INNER)), full((1, SSD_INNER)), full((LANES, SSD_INNER)),
        ],
        out_specs=[
            pl.BlockSpec((bg, q, SSD_INNER), lambda i, c: (i, c, 0)),
            pl.BlockSpec((bg, CONV_W - 1, CONV_DIM), lambda i, c: (i, 0, 0)),
            pl.BlockSpec((bg, SSD_HEADS, SSD_HEADDIM, SSD_STATE), lambda i, c: (i, 0, 0, 0)),
        ],
        out_shape=[
            jax.ShapeDtypeStruct((nb, length, SSD_INNER), BF16),
            jax.ShapeDtypeStruct((nb, CONV_W - 1, CONV_DIM), F32),
            jax.ShapeDtypeStruct((nb, SSD_HEADS, SSD_HEADDIM, SSD_STATE), F32),
        ],
        scratch_shapes=[
            pltpu.VMEM((bg, q + 8, CONV_DIM), F32),
            pltpu.VMEM((bg, SSD_HEADS, SSD_HEADDIM, SSD_STATE), F32),
            pltpu.VMEM((rows, LANES), F32),
            pltpu.VMEM((rows, LANES), F32),
            pltpu.VMEM((SSD_GROUPS, gw, rows), BF16),
            pltpu.VMEM((SSD_GROUPS, gw, rows), F32),
            pltpu.VMEM((rows, SSD_INNER), F32),
        ],
        compiler_params=_cparams(("parallel", "arbitrary")),
        name="ssd",
    )(p3, p3, p3, p3, conv_prev, h0, conv_w, conv_b, dtb, alog, dsk_x, norm_w, eexp)


def _gla_body(q_ref, k_ref, v_ref, r_ref, dtg_ref, s0_ref, w2_ref, b2_ref, nw_ref, o_ref, sout_ref,
              sst, qe_scr, kdt_scr, v_scr, cdt_scr, oint_scr, *, bg, q):
    rows = bg * q
    c = pl.program_id(1)

    @pl.when(c == 0)
    def _():
        sst[...] = s0_ref[...]

    qv = q_ref[...].reshape(rows, GLA_KEY)
    kv = k_ref[...].reshape(rows, GLA_KEY)
    vv = v_ref[...].reshape(rows, GLA_VAL)
    glr = dtg_ref[...].reshape(rows, LANES)
    g = jax.nn.log_sigmoid(_dot(glr.astype(BF16), w2_ref[...]) + b2_ref[...]) * (1.0 / GLA_GATE_NORM)

    same, causal = _block_masks(rows, q)
    gc = _dot_hi(causal.astype(F32), g)
    gl = _dot_hi(same.astype(F32), g)
    qe = qv * (GLA_DK ** -0.5) * jnp.exp(gc)
    ke = kv * jnp.exp(-gc)
    kd = kv * jnp.exp(gl - gc)
    cdk = jnp.exp(gl)

    qe_b = qe.astype(BF16)
    ke_b = ke.astype(BF16)
    v_b = vv.astype(BF16)
    o_intra = []
    for h in range(GLA_HEADS):
        ks = slice(h * GLA_DK, (h + 1) * GLA_DK)
        att = jnp.where(causal, _dot_nt(qe_b[:, ks], ke_b[:, ks]), 0.0)
        o_intra.append(_dot(att.astype(BF16), v_b[:, h * GLA_DV:(h + 1) * GLA_DV]))
        kdt_scr[h] = kd[:, ks].T.astype(BF16)
        cdt_scr[h] = cdk[:, ks].T
    qe_scr[...] = qe
    v_scr[...] = vv

    def seq_step(b, carry):
        r0 = pl.multiple_of(b * q, q)
        rowm, sel = _row_select(rows, GLA_DV, r0, q)
        for h in range(GLA_HEADS):
            s_old = sst[b, h]
            qeb = qe_scr[pl.ds(r0, q), h * GLA_DK:(h + 1) * GLA_DK]
            oint_scr[pl.ds(r0, q), h * GLA_DV:(h + 1) * GLA_DV] = _dot(qeb.astype(BF16), s_old.astype(BF16))
            vm = jnp.where(rowm, v_scr[:, h * GLA_DV:(h + 1) * GLA_DV], 0.0).astype(BF16)
            s_new = _dot(kdt_scr[h], vm)
            cd = _dot_hi(cdt_scr[h], sel)
            sst[b, h] = cd * s_old + s_new
        return carry

    lax.fori_loop(0, bg, seq_step, 0)

    rv = r_ref[...].reshape(rows, GLA_VAL)
    outs = []
    for h in range(GLA_HEADS):
        vs = slice(h * GLA_DV, (h + 1) * GLA_DV)
        o = o_intra[h] + oint_scr[:, vs]
        ms = jnp.mean(o * o, axis=-1, keepdims=True)
        outs.append(o * lax.rsqrt(ms + EPS) * nw_ref[...])
    o_all = jnp.concatenate(outs, axis=1) * _silu(rv)
    o_ref[...] = o_all.astype(BF16).reshape(bg, q, GLA_VAL)

    @pl.when(c == pl.num_programs(1) - 1)
    def _():
        sout_ref[...] = sst[...]


def _gla_call(p3, s0, w2_pad, b2, norm_w, bg, q):
    nb, length, _ = p3.shape
    full = lambda shape: pl.BlockSpec(shape, lambda i, c: (0,) * len(shape))
    rows = bg * q
    state_block = (bg, GLA_HEADS, GLA_DK, GLA_DV)
    return pl.pallas_call(
        functools.partial(_gla_body, bg=bg, q=q),
        grid=(nb // bg, length // q),
        in_specs=[
            pl.BlockSpec((bg, q, GLA_KEY), lambda i, c: (i, c, COL_Q // GLA_KEY)),
            pl.BlockSpec((bg, q, GLA_KEY), lambda i, c: (i, c, COL_K // GLA_KEY)),
            pl.BlockSpec((bg, q, GLA_VAL), lambda i, c: (i, c, COL_V // GLA_VAL)),
            pl.BlockSpec((bg, q, GLA_VAL), lambda i, c: (i, c, COL_R // GLA_VAL)),
            pl.BlockSpec((bg, q, LANES), lambda i, c: (i, c, COL_DTG // LANES)),
            pl.BlockSpec(state_block, lambda i, c: (i, 0, 0, 0)),
            full((LANES, GLA_KEY)), full((1, GLA_KEY)), full((1, GLA_DV)),
        ],
        out_specs=[
            pl.BlockSpec((bg, q, GLA_VAL), lambda i, c: (i, c, 0)),
            pl.BlockSpec(state_block, lambda i, c: (i, 0, 0, 0)),
        ],
        out_shape=[
            jax.ShapeDtypeStruct((nb, length, GLA_VAL), BF16),
            jax.ShapeDtypeStruct((nb, GLA_HEADS, GLA_DK, GLA_DV), F32),
        ],
        scratch_shapes=[
            pltpu.VMEM(state_block, F32),
            pltpu.VMEM((rows, GLA_KEY), F32),
            pltpu.VMEM((GLA_HEADS, GLA_DK, rows), BF16),
            pltpu.VMEM((rows, GLA_VAL), F32),
            pltpu.VMEM((GLA_HEADS, GLA_DK, rows), F32),
            pltpu.VMEM((rows, GLA_VAL), F32),
        ],
        compiler_params=_cparams(("parallel", "arbitrary")),
        name="gla",
    )(p3, p3, p3, p3, p3, s0, w2_pad, b2, norm_w)


def _merge_body(x_ref, ys_ref, o_ref, gs_ref, gg_ref, wso_ref, wgo_ref, wo_ref, nw_ref, x1_ref, h2t_ref):
    mix = (jax.nn.sigmoid(gs_ref[...]) * _dot(ys_ref[...], wso_ref[...])
           + jax.nn.sigmoid(gg_ref[...]) * _dot(o_ref[...], wgo_ref[...]))
    x1 = x_ref[...] + _dot(mix.astype(BF16), wo_ref[...])
    x1_ref[...] = x1
    ms = jnp.mean(x1 * x1, axis=-1, keepdims=True)
    h2 = x1 * lax.rsqrt(ms + EPS) * nw_ref[...]
    h2t_ref[...] = h2.T.astype(BF16)


def _merge_call(x2d, ys, o, proj, wso, wgo, wo, norm_w, tm):
    n = x2d.shape[0]
    tok = lambda col: pl.BlockSpec((tm, D_MODEL), lambda i: (i, col))
    wfull = pl.BlockSpec((D_MODEL, D_MODEL), lambda i: (0, 0))
    return pl.pallas_call(
        _merge_body,
        grid=(n // tm,),
        in_specs=[tok(0), tok(0), tok(0), tok(COL_GS // D_MODEL), tok(COL_GG // D_MODEL),
                  wfull, wfull, wfull, pl.BlockSpec((1, D_MODEL), lambda i: (0, 0))],
        out_specs=[pl.BlockSpec((tm, D_MODEL), lambda i: (i, 0)),
                   pl.BlockSpec((D_MODEL, tm), lambda i: (0, i))],
        out_shape=[jax.ShapeDtypeStruct((n, D_MODEL), F32),
                   jax.ShapeDtypeStruct((D_MODEL, n), BF16)],
        compiler_params=_cparams(("parallel",)),
        name="merge",
    )(x2d, ys, o, proj, proj, wso, wgo, wo, norm_w)


def _top16_rows(s):
    row = lax.broadcasted_iota(jnp.int32, (PEER_TOPK, s.shape[1]), 0)
    vals = jnp.zeros((PEER_TOPK, s.shape[1]), F32)
    cur = s
    for it in range(PEER_TOPK):
        m = jnp.max(cur, axis=0, keepdims=True)
        vals = jnp.where(row == it, m, vals)
        cur = jnp.where(cur == m, -jnp.inf, cur)
    return vals


def _topk_body(h2t_ref, wqt_ref, keys_ref, s2_ref, e2_ref, th_ref, c_ref):
    tt = h2t_ref.shape[1]
    qt = _dot(wqt_ref[...], h2t_ref[...])
    row16 = lax.broadcasted_iota(jnp.int32, (PEER_TOPK, tt), 0)
    row8 = lax.broadcasted_iota(jnp.int32, (8, tt), 0)
    neg = -jnp.inf
    for h in range(PEER_HEADS):
        base = h * 2 * PEER_HALF
        s1 = _dot(keys_ref[h, 0], qt[base:base + PEER_HALF].astype(BF16))
        s2 = _dot(keys_ref[h, 1], qt[base + PEER_HALF:base + 2 * PEER_HALF].astype(BF16))
        v1 = _top16_rows(s1)
        v2 = _top16_rows(s2)
        cands = [v1[0:1] + v2]
        v2lo = v2[0:8]
        for a, nb in ((1, 8), (2, 5), (3, 4), (4, 3), (5, 2), (6, 2), (7, 2)):
            cands.append(jnp.where(row8 < nb, v1[a:a + 1] + v2lo, neg))
        cands.append(v1[8:16] + v2[0:1])
        cur = jnp.concatenate(cands, axis=0)
        top = v1[0:1] + v2[0:1]
        zsum = jnp.zeros((1, tt), F32)
        tau = top
        count = jnp.zeros((1, tt), F32)
        for _ in range(PEER_TOPK):
            m = jnp.max(cur, axis=0, keepdims=True)
            eq = cur == m
            n_eq = jnp.sum(jnp.where(eq, 1.0, 0.0), axis=0, keepdims=True)
            need = count < PEER_TOPK
            tau = jnp.where(need, m, tau)
            zsum = zsum + jnp.where(need, n_eq * jnp.exp(m - top), 0.0)
            count = count + n_eq
            cur = jnp.where(eq, neg, cur)
        th = jnp.full((N_KEYS, tt), jnp.inf, F32)
        for b in range(PEER_TOPK):
            vb = v2[b:b + 1]
            th = jnp.where(s1 + vb >= tau, vb, th)
        s2_ref[h] = s2
        e2_ref[h] = jnp.exp(s2 - v2[0:1])
        th_ref[h] = th
        c_ref[h] = jnp.exp(s1 - v1[0:1]) / zsum


def _topk_call(h2t, wqt, keys, tt):
    n = h2t.shape[1]
    tab = pl.BlockSpec((PEER_HEADS, N_KEYS, tt), lambda i: (0, 0, i))
    tab_shape = jax.ShapeDtypeStruct((PEER_HEADS, N_KEYS, n), F32)
    return pl.pallas_call(
        _topk_body,
        grid=(n // tt,),
        in_specs=[pl.BlockSpec((D_MODEL, tt), lambda i: (0, i)),
                  pl.BlockSpec(wqt.shape, lambda i: (0, 0)),
                  pl.BlockSpec(keys.shape, lambda i: (0, 0, 0, 0))],
        out_specs=[tab, tab, tab, tab],
        out_shape=[tab_shape] * 4,
        compiler_params=_cparams(("parallel",)),
        name="topk",
    )(h2t, wqt, keys)


def _gelu(x):
    return 0.5 * x * (1.0 + lax.erf(x * (2.0 ** -0.5)))


def _peer_body(h2t_ref, x1_ref, s2_ref, e2_ref, th_ref, c_ref, u_ref, vt_ref, nw_ref, y_ref,
               acc_scr, w_scr, *, ec):
    e = pl.program_id(1)
    groups = ec // N_KEYS

    @pl.when(e == 0)
    def _():
        acc_scr[...] = jnp.zeros_like(acc_scr)

    act = _gelu(_dot(u_ref[...], h2t_ref[...]))
    for ii in range(groups):
        i_key = e * groups + ii
        gate = None
        for h in range(PEER_HEADS):
            th_row = th_ref[h, pl.ds(i_key, 1), :]
            c_row = c_ref[h, pl.ds(i_key, 1), :]
            term = jnp.where(s2_ref[h] >= th_row, e2_ref[h], 0.0) * c_row
            gate = term if gate is None else gate + term
        w_scr[ii * N_KEYS:(ii + 1) * N_KEYS, :] = (act[ii * N_KEYS:(ii + 1) * N_KEYS] * gate).astype(BF16)
    acc_scr[...] += _dot(vt_ref[...], w_scr[...])

    @pl.when(e == pl.num_programs(1) - 1)
    def _():
        x2 = x1_ref[...] + acc_scr[...].T
        ms = jnp.mean(x2 * x2, axis=-1, keepdims=True)
        y_ref[...] = x2 * lax.rsqrt(ms + EPS) * nw_ref[...]


def _peer_call(h2t, x1, s2, e2, th, cc, u_b, vt_b, norm_w, tt, ec):
    n = x1.shape[0]
    n_exp = u_b.shape[0]
    tab = pl.BlockSpec((PEER_HEADS, N_KEYS, tt), lambda i, e: (0, 0, i))
    return pl.pallas_call(
        functools.partial(_peer_body, ec=ec),
        grid=(n // tt, n_exp // ec),
        in_specs=[pl.BlockSpec((D_MODEL, tt), lambda i, e: (0, i)),
                  pl.BlockSpec((tt, D_MODEL), lambda i, e: (i, 0)),
                  tab, tab, tab, tab,
                  pl.BlockSpec((ec, D_MODEL), lambda i, e: (e, 0)),
                  pl.BlockSpec((D_MODEL, ec), lambda i, e: (0, e)),
                  pl.BlockSpec((1, D_MODEL), lambda i, e: (0, 0))],
        out_specs=pl.BlockSpec((tt, D_MODEL), lambda i, e: (i, 0)),
        out_shape=jax.ShapeDtypeStruct((n, D_MODEL), F32),
        scratch_shapes=[pltpu.VMEM((D_MODEL, tt), F32), pltpu.VMEM((ec, tt), BF16)],
        compiler_params=_cparams(("parallel", "arbitrary")),
        name="peer",
    )(h2t, x1, s2, e2, th, cc, u_b, vt_b, norm_w)


def _permute_w_in(w):
    splits = [sum(IN_SIZES[:i + 1]) for i in range(len(IN_SIZES) - 1)]
    z, xbc, dt, q, k, v, glr, r, gs, gg = jnp.split(w, splits, axis=1)
    pad = jnp.zeros((D_MODEL, PROJ_COLS - COL_DTG - SSD_HEADS - GLA_GATE_RANK), w.dtype)
    return jnp.concatenate([z, v, r, gs, gg, q, k, xbc, dt, glr, pad], axis=1).astype(BF16)


def _row(v, width=None):
    v = v.reshape(1, -1).astype(F32)
    if width is not None and v.shape[1] < width:
        v = jnp.pad(v, ((0, 0), (0, width - v.shape[1])))
    return v


def _trunk(x, conv0, ssd0, gla0, wts, bg_ssd, q_ssd, bg_gla, q_gla, tm, tt, ec):
    nb, length, _ = x.shape
    n = nb * length
    x2d = x.reshape(n, D_MODEL)
    proj = _proj_call(x2d, wts["norm_mix"], wts["w_in"], tm, 1280)
    p3 = proj.reshape(nb, length, PROJ_COLS)
    ys, conv_new, ssd_new = _ssd_call(p3, conv0, ssd0, wts["conv_w"], wts["conv_b"], wts["dtb"], wts["alog"],
                                      wts["dsk"], wts["ssd_norm"], wts["eexp"], bg_ssd, q_ssd)
    o, gla_new = _gla_call(p3, gla0, wts["w2"], wts["b2"], wts["gla_norm"], bg_gla, q_gla)
    x1, h2t = _merge_call(x2d, ys.reshape(n, SSD_INNER), o.reshape(n, GLA_VAL), proj,
                          wts["wso"], wts["wgo"], wts["wo"], wts["norm_ffn"], tm // 2)
    s2, e2, th, cc = _topk_call(h2t, wts["wqt"], wts["keys"], tt)
    y = _peer_call(h2t, x1, s2, e2, th, cc, wts["u"], wts["vt"], wts["final_norm"], tt, ec)
    return y.reshape(nb, length, D_MODEL), conv_new[None], ssd_new[None], gla_new[None]


def _weights(p):
    assert p["norm_mix_w"].shape[0] == 1, "one layer"
    w2_pad = jnp.zeros((LANES, GLA_KEY), F32).at[SSD_HEADS:SSD_HEADS + GLA_GATE_RANK].set(p["w_gla_gate2"][0])
    head_of_lane = jnp.arange(SSD_INNER) // SSD_HEADDIM
    return {
        "norm_mix": _row(p["norm_mix_w"][0]),
        "w_in": _permute_w_in(p["w_in"][0]),
        "conv_w": p["conv_w"][0], "conv_b": _row(p["conv_b"][0]),
        "dtb": _row(p["dt_bias"][0], LANES), "alog": _row(p["a_log"][0], LANES),
        "dsk": _row(jnp.repeat(p["d_skip"][0], SSD_HEADDIM)),
        "ssd_norm": _row(p["ssd_norm_w"][0]),
        "eexp": (jnp.arange(LANES)[:, None] == head_of_lane[None, :]).astype(F32),
        "w2": w2_pad.astype(BF16), "b2": _row(p["b_gla_gate"][0]), "gla_norm": _row(p["gla_norm_w"][0]),
        "wso": p["w_ssd_out"][0].astype(BF16), "wgo": p["w_gla_out"][0].astype(BF16),
        "wo": p["w_out"][0].astype(BF16),
        "norm_ffn": _row(p["norm_ffn_w"][0]),
        "wqt": p["w_query"][0].T.astype(BF16), "keys": p["sub_keys"][0].astype(BF16),
        "u": p["expert_u"][0].astype(BF16), "vt": p["expert_v"][0].T.astype(BF16),
        "final_norm": _row(p["final_norm_w"]),
    }


def kernel(x_prompt, x_sample, state_conv, state_ssd, state_gla, norm_mix_w, w_in, conv_w, conv_b, dt_bias,
           a_log, d_skip, ssd_norm_w, w_gla_gate2, b_gla_gate, gla_norm_w, w_ssd_out, w_gla_out, w_out,
           norm_ffn_w, w_query, sub_keys, expert_u, expert_v, final_norm_w):
    wts = _weights(dict(
        norm_mix_w=norm_mix_w, w_in=w_in, conv_w=conv_w, conv_b=conv_b, dt_bias=dt_bias, a_log=a_log,
        d_skip=d_skip, ssd_norm_w=ssd_norm_w, w_gla_gate2=w_gla_gate2, b_gla_gate=b_gla_gate,
        gla_norm_w=gla_norm_w, w_ssd_out=w_ssd_out, w_gla_out=w_gla_out, w_out=w_out, norm_ffn_w=norm_ffn_w,
        w_query=w_query, sub_keys=sub_keys, expert_u=expert_u, expert_v=expert_v, final_norm_w=final_norm_w))
    bp = x_prompt.shape[0]
    conv0 = jnp.zeros((bp, CONV_W - 1, CONV_DIM), F32)
    ssd0 = jnp.zeros((bp, SSD_HEADS, SSD_HEADDIM, SSD_STATE), F32)
    gla0 = jnp.zeros((bp, GLA_HEADS, GLA_DK, GLA_DV), F32)
    ls = x_sample.shape[1]
    y_p, conv_p, ssd_p, gla_p = _trunk(x_prompt, conv0, ssd0, gla0, wts,
                                       bg_ssd=1, q_ssd=ROWS, bg_gla=2, q_gla=ROWS // 2, tm=1024, tt=512, ec=1024)
    y_s, conv_s, ssd_s, gla_s = _trunk(x_sample, state_conv[0], state_ssd[0], state_gla[0], wts,
                                       bg_ssd=ROWS // ls, q_ssd=ls, bg_gla=ROWS // ls, q_gla=ls,
                                       tm=1024, tt=512, ec=1024)
    return (y_p, y_s, conv_p, ssd_p, gla_p, conv_s, ssd_s, gla_s)
```

```python
import functools
import math

import jax
import jax.numpy as jnp
from jax import lax
from jax.experimental import pallas as pl
from jax.experimental.pallas import tpu as pltpu

F32 = jnp.float32
BF16 = jnp.bfloat16
HIGHEST = lax.Precision.HIGHEST

D_MODEL = 1024
SSD_HEADS = 16
SSD_HEADDIM = 64
SSD_INNER = 1024
SSD_GROUPS = 2
SSD_STATE = 64
CONV_W = 4
CONV_DIM = 1280
GLA_HEADS = 4
GLA_DK = 128
GLA_DV = 256
GLA_KEY = 512
GLA_VAL = 1024
GLA_GATE_RANK = 16
GLA_GATE_NORM = 16.0
IN_SIZES = (SSD_INNER, CONV_DIM, SSD_HEADS, GLA_KEY, GLA_KEY, GLA_VAL, GLA_GATE_RANK, GLA_VAL, D_MODEL, D_MODEL)
PEER_HEADS = 8
N_KEYS = 128
PEER_HALF = 128
PEER_TOPK = 16
EPS = 1e-6

LANES = 128
ROWS = 128
VMEM_LIMIT = 56 * 1024 * 1024

COL_Z, COL_V, COL_R, COL_GS, COL_GG = 0, 1024, 2048, 3072, 4096
COL_Q, COL_K, COL_XS, COL_BC, COL_DTG = 5120, 5632, 6144, 7168, 7424
PROJ_COLS = 7680

NT_DIMS = (((1,), (1,)), ((), ()))


def _cparams(sem):
    return pltpu.CompilerParams(dimension_semantics=sem, vmem_limit_bytes=VMEM_LIMIT)


def _dot(a, b):
    return jnp.dot(a, b, preferred_element_type=F32)


def _dot_hi(a, b):
    return jnp.dot(a, b, preferred_element_type=F32, precision=HIGHEST)


def _dot_nt(a, b):
    return lax.dot_general(a, b, NT_DIMS, preferred_element_type=F32)


def _silu(x):
    return x * jax.nn.sigmoid(x)


def _proj_body(x_ref, nw_ref, w_ref, o_ref, h_scr):
    @pl.when(pl.program_id(1) == 0)
    def _():
        x = x_ref[...]
        ms = jnp.mean(x * x, axis=-1, keepdims=True)
        h_scr[...] = (x * lax.rsqrt(ms + EPS) * nw_ref[...]).astype(BF16)

    o_ref[...] = _dot(h_scr[...], w_ref[...])


def _proj_call(x2d, norm_w, w_perm, tm, tn):
    n = x2d.shape[0]
    return pl.pallas_call(
        _proj_body,
        grid=(n // tm, PROJ_COLS // tn),
        in_specs=[
            pl.BlockSpec((tm, D_MODEL), lambda i, j: (i, 0)),
            pl.BlockSpec((1, D_MODEL), lambda i, j: (0, 0)),
            pl.BlockSpec((D_MODEL, tn), lambda i, j: (0, j)),
        ],
        out_specs=pl.BlockSpec((tm, tn), lambda i, j: (i, j)),
        out_shape=jax.ShapeDtypeStruct((n, PROJ_COLS), F32),
        scratch_shapes=[pltpu.VMEM((tm, D_MODEL), BF16)],
        compiler_params=_cparams(("parallel", "arbitrary")),
        name="proj",
    )(x2d, norm_w, w_perm)


def _block_masks(rows, q):
    ri = lax.broadcasted_iota(jnp.int32, (rows, rows), 0)
    ci = lax.broadcasted_iota(jnp.int32, (rows, rows), 1)
    shift = int(math.log2(q))
    same = (ri >> shift) == (ci >> shift)
    causal = same & (ci <= ri)
    return same, causal


def _row_select(rows, width, r0, q):
    rcol = lax.broadcasted_iota(jnp.int32, (rows, 1), 0)
    rowm = (rcol >= r0) & (rcol < r0 + q)
    rfull = lax.broadcasted_iota(jnp.int32, (rows, width), 0)
    sel = jnp.where(rfull == r0 + (q - 1), 1.0, 0.0).astype(F32)
    return rowm, sel


def _ssd_body(xs_ref, bc_ref, dtg_ref, z_ref, cprev_ref, h0_ref, cw_ref, cb_ref, dtb_ref, alog_ref,
              dsk_ref, nw_ref, eexp_ref, ys_ref, cnew_ref, hout_ref,
              xext, hst, c_scr, b_scr, xwt_scr, cdt_scr, yoff_scr, *, bg, q):
    rows = bg * q
    c = pl.program_id(1)
    ngroup = SSD_HEADS // SSD_GROUPS
    gw = ngroup * SSD_HEADDIM

    @pl.when(c == 0)
    def _():
        xext[:, 5:8, :] = cprev_ref[...]
        hst[...] = h0_ref[...]

    xext[:, 8:8 + q, 0:SSD_INNER] = xs_ref[...]
    xext[:, 8:8 + q, SSD_INNER:CONV_DIM] = bc_ref[...]
    acc = cb_ref[...][None]
    for k in range(CONV_W):
        acc = acc + cw_ref[k:k + 1, :][None] * xext[:, 5 + k:5 + k + q, :]
    xc = _silu(acc).reshape(rows, CONV_DIM)
    tail = xext[:, q + 5:q + 8, :]
    cnew_ref[...] = tail
    xext[:, 5:8, :] = tail

    xs = xc[:, :SSD_INNER]
    bm = xc[:, SSD_INNER:SSD_INNER + LANES]
    cm = xc[:, SSD_INNER + LANES:CONV_DIM]

    lane = lax.broadcasted_iota(jnp.int32, (1, LANES), 1)
    hmask = lane < SSD_HEADS
    dt = jnp.where(hmask, jax.nn.softplus(dtg_ref[...].reshape(rows, LANES) + dtb_ref[...]), 0.0)
    a = dt * jnp.where(hmask, -jnp.exp(alog_ref[...]), 0.0)

    same, causal = _block_masks(rows, q)
    acum = _dot_hi(causal.astype(F32), a)
    alast = _dot_hi(same.astype(F32), a)

    eexp = eexp_ref[...]
    acum_x = _dot_hi(acum, eexp)
    alast_x = _dot_hi(alast, eexp)
    xdt = xs * _dot_hi(dt, eexp)
    xw = xdt * jnp.exp(alast_x - acum_x)

    cbs = []
    for g in range(SSD_GROUPS):
        cb = _dot_nt(cm[:, g * SSD_STATE:(g + 1) * SSD_STATE].astype(BF16),
                     bm[:, g * SSD_STATE:(g + 1) * SSD_STATE].astype(BF16))
        cbs.append(jnp.where(causal, cb, 0.0))
    ydiag = []
    for h in range(SSD_HEADS):
        col = jnp.broadcast_to(acum[:, h:h + 1], (rows, rows))
        dec = jnp.exp(jnp.where(causal, col - col.T, 0.0))
        w = (cbs[h // ngroup] * dec).astype(BF16)
        ydiag.append(_dot(w, xdt[:, h * SSD_HEADDIM:(h + 1) * SSD_HEADDIM].astype(BF16)))
    y = jnp.concatenate(ydiag, axis=1)

    c_scr[...] = cm
    b_scr[...] = bm
    cd_x = jnp.exp(alast_x)
    for g in range(SSD_GROUPS):
        xwt_scr[g] = xw[:, g * gw:(g + 1) * gw].T.astype(BF16)
        cdt_scr[g] = cd_x[:, g * gw:(g + 1) * gw].T

    def seq_step(b, carry):
        r0 = pl.multiple_of(b * q, q)
        rowm, sel = _row_select(rows, SSD_STATE, r0, q)
        for g in range(SSD_GROUPS):
            hg = hst[b, g * ngroup:(g + 1) * ngroup].reshape(gw, SSD_STATE)
            cg = c_scr[pl.ds(r0, q), g * SSD_STATE:(g + 1) * SSD_STATE]
            yoff_scr[pl.ds(r0, q), g * gw:(g + 1) * gw] = _dot_nt(cg.astype(BF16), hg.astype(BF16))
            bmask = jnp.where(rowm, b_scr[:, g * SSD_STATE:(g + 1) * SSD_STATE], 0.0).astype(BF16)
            s_new = _dot(xwt_scr[g], bmask)
            cd = _dot_hi(cdt_scr[g], sel)
            hst[b, g * ngroup:(g + 1) * ngroup] = (cd * hg + s_new).reshape(ngroup, SSD_HEADDIM, SSD_STATE)
        return carry

    lax.fori_loop(0, bg, seq_step, 0)

    y = y + yoff_scr[...] * jnp.exp(acum_x) + dsk_ref[...] * xs
    yz = y * _silu(z_ref[...].reshape(rows, SSD_INNER))
    ms = jnp.mean(yz * yz, axis=-1, keepdims=True)
    ys_ref[...] = (yz * lax.rsqrt(ms + EPS) * nw_ref[...]).astype(BF16).reshape(bg, q, SSD_INNER)

    @pl.when(c == pl.num_programs(1) - 1)
    def _():
        hout_ref[...] = hst[...]


def _ssd_call(p3, conv_prev, h0, conv_w, conv_b, dtb, alog, dsk_x, norm_w, eexp, bg, q):
    nb, length, _ = p3.shape
    full = lambda shape: pl.BlockSpec(shape, lambda i, c: (0,) * len(shape))
    rows = bg * q
    gw = SSD_INNER // SSD_GROUPS
    return pl.pallas_call(
        functools.partial(_ssd_body, bg=bg, q=q),
        grid=(nb // bg, length // q),
        in_specs=[
            pl.BlockSpec((bg, q, SSD_INNER), lambda i, c: (i, c, COL_XS // SSD_INNER)),
            pl.BlockSpec((bg, q, 2 * LANES), lambda i, c: (i, c, COL_BC // (2 * LANES))),
            pl.BlockSpec((bg, q, LANES), lambda i, c: (i, c, COL_DTG // LANES)),
            pl.BlockSpec((bg, q, SSD_INNER), lambda i, c: (i, c, COL_Z // SSD_INNER)),
            pl.BlockSpec((bg, CONV_W - 1, CONV_DIM), lambda i, c: (i, 0, 0)),
            pl.BlockSpec((bg, SSD_HEADS, SSD_HEADDIM, SSD_STATE), lambda i, c: (i, 0, 0, 0)),
            full((CONV_W, CONV_DIM)), full((1, CONV_DIM)), full((1, LANES)), full((1, LANES)),
            full((1, SSD_INNER)), full((1, SSD_INNER)), full((LANES, SSD_INNER)),
        ],
        out_specs=[
            pl.BlockSpec((bg, q, SSD_INNER), lambda i, c: (i, c, 0)),
            pl.BlockSpec((bg, CONV_W - 1, CONV_DIM), lambda i, c: (i, 0, 0)),
            pl.BlockSpec((bg, SSD_HEADS, SSD_HEADDIM, SSD_STATE), lambda i, c: (i, 0, 0, 0)),
        ],
        out_shape=[
            jax.ShapeDtypeStruct((nb, length, SSD_INNER), BF16),
            jax.ShapeDtypeStruct((nb, CONV_W - 1, CONV_DIM), F32),
            jax.ShapeDtypeStruct((nb, SSD_HEADS, SSD_HEADDIM, SSD_STATE), F32),
        ],
        scratch_shapes=[
            pltpu.VMEM((bg, q + 8, CONV_DIM), F32),
            pltpu.VMEM((bg, SSD_HEADS, SSD_HEADDIM, SSD_STATE), F32),
            pltpu.VMEM((rows, LANES), F32),
            pltpu.VMEM((rows, LANES), F32),
            pltpu.VMEM((SSD_GROUPS, gw, rows), BF16),
            pltpu.VMEM((SSD_GROUPS, gw, rows), F32),
            pltpu.VMEM((rows, SSD_INNER), F32),
        ],
        compiler_params=_cparams(("parallel", "arbitrary")),
        name="ssd",
    )(p3, p3, p3, p3, conv_prev, h0, conv_w, conv_b, dtb, alog, dsk_x, norm_w, eexp)


def _gla_body(q_ref, k_ref, v_ref, r_ref, dtg_ref, s0_ref, w2_ref, b2_ref, nw_ref, o_ref, sout_ref,
              sst, qe_scr, kdt_scr, v_scr, cdt_scr, oint_scr, *, bg, q):
    rows = bg * q
    c = pl.program_id(1)

    @pl.when(c == 0)
    def _():
        sst[...] = s0_ref[...]

    qv = q_ref[...].reshape(rows, GLA_KEY)
    kv = k_ref[...].reshape(rows, GLA_KEY)
    vv = v_ref[...].reshape(rows, GLA_VAL)
    glr = dtg_ref[...].reshape(rows, LANES)
    g = jax.nn.log_sigmoid(_dot(glr.astype(BF16), w2_ref[...]) + b2_ref[...]) * (1.0 / GLA_GATE_NORM)

    same, causal = _block_masks(rows, q)
    gc = _dot_hi(causal.astype(F32), g)
    gl = _dot_hi(same.astype(F32), g)
    qe = qv * (GLA_DK ** -0.5) * jnp.exp(gc)
    ke = kv * jnp.exp(-gc)
    kd = kv * jnp.exp(gl - gc)
    cdk = jnp.exp(gl)

    qe_b = qe.astype(BF16)
    ke_b = ke.astype(BF16)
    v_b = vv.astype(BF16)
    o_intra = []
    for h in range(GLA_HEADS):
        ks = slice(h * GLA_DK, (h + 1) * GLA_DK)
        att = jnp.where(causal, _dot_nt(qe_b[:, ks], ke_b[:, ks]), 0.0)
        o_intra.append(_dot(att.astype(BF16), v_b[:, h * GLA_DV:(h + 1) * GLA_DV]))
        kdt_scr[h] = kd[:, ks].T.astype(BF16)
        cdt_scr[h] = cdk[:, ks].T
    qe_scr[...] = qe
    v_scr[...] = vv

    def seq_step(b, carry):
        r0 = pl.multiple_of(b * q, q)
        rowm, sel = _row_select(rows, GLA_DV, r0, q)
        for h in range(GLA_HEADS):
            s_old = sst[b, h]
            qeb = qe_scr[pl.ds(r0, q), h * GLA_DK:(h + 1) * GLA_DK]
            oint_scr[pl.ds(r0, q), h * GLA_DV:(h + 1) * GLA_DV] = _dot(qeb.astype(BF16), s_old.astype(BF16))
            vm = jnp.where(rowm, v_scr[:, h * GLA_DV:(h + 1) * GLA_DV], 0.0).astype(BF16)
            s_new = _dot(kdt_scr[h], vm)
            cd = _dot_hi(cdt_scr[h], sel)
            sst[b, h] = cd * s_old + s_new
        return carry

    lax.fori_loop(0, bg, seq_step, 0)

    rv = r_ref[...].reshape(rows, GLA_VAL)
    outs = []
    for h in range(GLA_HEADS):
        vs = slice(h * GLA_DV, (h + 1) * GLA_DV)
        o = o_intra[h] + oint_scr[:, vs]
        ms = jnp.mean(o * o, axis=-1, keepdims=True)
        outs.append(o * lax.rsqrt(ms + EPS) * nw_ref[...])
    o_all = jnp.concatenate(outs, axis=1) * _silu(rv)
    o_ref[...] = o_all.astype(BF16).reshape(bg, q, GLA_VAL)

    @pl.when(c == pl.num_programs(1) - 1)
    def _():
        sout_ref[...] = sst[...]


def _gla_call(p3, s0, w2_pad, b2, norm_w, bg, q):
    nb, length, _ = p3.shape
    full = lambda shape: pl.BlockSpec(shape, lambda i, c: (0,) * len(shape))
    rows = bg * q
    state_block = (bg, GLA_HEADS, GLA_DK, GLA_DV)
    return pl.pallas_call(
        functools.partial(_gla_body, bg=bg, q=q),
        grid=(nb // bg, length // q),
        in_specs=[
            pl.BlockSpec((bg, q, GLA_KEY), lambda i, c: (i, c, COL_Q // GLA_KEY)),
            pl.BlockSpec((bg, q, GLA_KEY), lambda i, c: (i, c, COL_K // GLA_KEY)),
            pl.BlockSpec((bg, q, GLA_VAL), lambda i, c: (i, c, COL_V // GLA_VAL)),
            pl.BlockSpec((bg, q, GLA_VAL), lambda i, c: (i, c, COL_R // GLA_VAL)),
            pl.BlockSpec((bg, q, LANES), lambda i, c: (i, c, COL_DTG // LANES)),
            pl.BlockSpec(state_block, lambda i, c: (i, 0, 0, 0)),
            full((LANES, GLA_KEY)), full((1, GLA_KEY)), full((1, GLA_DV)),
        ],
        out_specs=[
            pl.BlockSpec((bg, q, GLA_VAL), lambda i, c: (i, c, 0)),
            pl.BlockSpec(state_block, lambda i, c: (i, 0, 0, 0)),
        ],
        out_shape=[
            jax.ShapeDtypeStruct((nb, length, GLA_VAL), BF16),
            jax.ShapeDtypeStruct((nb, GLA_HEADS, GLA_DK, GLA_DV), F32),
        ],
        scratch_shapes=[
            pltpu.VMEM(state_block, F32),
            pltpu.VMEM((rows, GLA_KEY), F32),
            pltpu.VMEM((GLA_HEADS, GLA_DK, rows), BF16),
            pltpu.VMEM((rows, GLA_VAL), F32),
            pltpu.VMEM((GLA_HEADS, GLA_DK, rows), F32),
            pltpu.VMEM((rows, GLA_VAL), F32),
        ],
        compiler_params=_cparams(("parallel", "arbitrary")),
        name="gla",
    )(p3, p3, p3, p3, p3, s0, w2_pad, b2, norm_w)


def _merge_body(x_ref, ys_ref, o_ref, gs_ref, gg_ref, wso_ref, wgo_ref, wo_ref, nw_ref, x1_ref, h2t_ref):
    mix = (jax.nn.sigmoid(gs_ref[...]) * _dot(ys_ref[...], wso_ref[...])
           + jax.nn.sigmoid(gg_ref[...]) * _dot(o_ref[...], wgo_ref[...]))
    x1 = x_ref[...] + _dot(mix.astype(BF16), wo_ref[...])
    x1_ref[...] = x1
    ms = jnp.mean(x1 * x1, axis=-1, keepdims=True)
    h2 = x1 * lax.rsqrt(ms + EPS) * nw_ref[...]
    h2t_ref[...] = h2.T.astype(BF16)


def _merge_call(x2d, ys, o, proj, wso, wgo, wo, norm_w, tm):
    n = x2d.shape[0]
    tok = lambda col: pl.BlockSpec((tm, D_MODEL), lambda i: (i, col))
    wfull = pl.BlockSpec((D_MODEL, D_MODEL), lambda i: (0, 0))
    return pl.pallas_call(
        _merge_body,
        grid=(n // tm,),
        in_specs=[tok(0), tok(0), tok(0), tok(COL_GS // D_MODEL), tok(COL_GG // D_MODEL),
                  wfull, wfull, wfull, pl.BlockSpec((1, D_MODEL), lambda i: (0, 0))],
        out_specs=[pl.BlockSpec((tm, D_MODEL), lambda i: (i, 0)),
                   pl.BlockSpec((D_MODEL, tm), lambda i: (0, i))],
        out_shape=[jax.ShapeDtypeStruct((n, D_MODEL), F32),
                   jax.ShapeDtypeStruct((D_MODEL, n), BF16)],
        compiler_params=_cparams(("parallel",)),
        name="merge",
    )(x2d, ys, o, proj, proj, wso, wgo, wo, norm_w)


def _top16_rows(s, want_rank):
    row = lax.broadcasted_iota(jnp.int32, (PEER_TOPK, s.shape[1]), 0)
    vals = jnp.zeros((PEER_TOPK, s.shape[1]), F32)
    rank = jnp.full(s.shape, float(PEER_TOPK), F32) if want_rank else None
    cur = s
    for it in range(PEER_TOPK):
        m = jnp.max(cur, axis=0, keepdims=True)
        vals = jnp.where(row == it, m, vals)
        eq = cur == m
        if want_rank:
            rank = jnp.where(eq, float(it), rank)
        cur = jnp.where(eq, -jnp.inf, cur)
    return vals, rank


def _next_up(x):
    x = x + 0.0
    bits = lax.bitcast_convert_type(x, jnp.int32)
    return lax.bitcast_convert_type(jnp.where(x >= 0.0, bits + 1, bits - 1), F32)


def _topk_body(h2t_ref, wqt_ref, keys_ref, rk2_ref, e2_ref, cnt_ref, c_ref):
    tt = h2t_ref.shape[1]
    qt = _dot(wqt_ref[...], h2t_ref[...])
    row8 = lax.broadcasted_iota(jnp.int32, (8, tt), 0)
    neg = -jnp.inf
    for h in range(PEER_HEADS):
        base = h * 2 * PEER_HALF
        s1 = _dot(keys_ref[h, 0], qt[base:base + PEER_HALF].astype(BF16))
        s2 = _dot(keys_ref[h, 1], qt[base + PEER_HALF:base + 2 * PEER_HALF].astype(BF16))
        v1, _ = _top16_rows(s1, False)
        v2, rank2 = _top16_rows(s2, True)
        cands = [v1[0:1] + v2]
        firsts = [jnp.broadcast_to(v1[0:1], (PEER_TOPK, tt))]
        v2lo = v2[0:8]
        for a, nb in ((1, 8), (2, 5), (3, 4), (4, 3), (5, 2), (6, 2), (7, 2)):
            cands.append(jnp.where(row8 < nb, v1[a:a + 1] + v2lo, neg))
            firsts.append(jnp.broadcast_to(v1[a:a + 1], (8, tt)))
        cands.append(v1[8:16] + v2[0:1])
        firsts.append(v1[8:16])
        cur = jnp.concatenate(cands, axis=0)
        first = jnp.concatenate(firsts, axis=0)
        top = v1[0:1] + v2[0:1]
        zsum = jnp.zeros((1, tt), F32)
        tau = top
        tau1 = v1[0:1]
        for _ in range(PEER_TOPK):
            tau = jnp.max(cur, axis=0, keepdims=True)
            eq = cur == tau
            tau1 = jnp.max(jnp.where(eq, first, neg), axis=0, keepdims=True)
            cur = jnp.where(eq, jnp.where(first == tau1, neg, cur), cur)
            zsum = zsum + jnp.exp(tau - top)
        tau_i = jnp.where(s1 >= tau1, tau, _next_up(tau))
        cnt = jnp.zeros((N_KEYS, tt), F32)
        for b in range(PEER_TOPK):
            cnt = cnt + jnp.where(s1 + v2[b:b + 1] >= tau_i, 1.0, 0.0)
        rk2_ref[h] = rank2.astype(BF16)
        e2_ref[h] = jnp.exp(s2 - v2[0:1]).astype(BF16)
        cnt_ref[h] = cnt
        c_ref[h] = jnp.exp(s1 - v1[0:1]) / zsum


def _topk_call(h2t, wqt, keys, tt):
    n = h2t.shape[1]
    tab = pl.BlockSpec((PEER_HEADS, N_KEYS, tt), lambda i: (0, 0, i))
    tab_f32 = jax.ShapeDtypeStruct((PEER_HEADS, N_KEYS, n), F32)
    tab_b16 = jax.ShapeDtypeStruct((PEER_HEADS, N_KEYS, n), BF16)
    return pl.pallas_call(
        _topk_body,
        grid=(n // tt,),
        in_specs=[pl.BlockSpec((D_MODEL, tt), lambda i: (0, i)),
                  pl.BlockSpec(wqt.shape, lambda i: (0, 0)),
                  pl.BlockSpec(keys.shape, lambda i: (0, 0, 0, 0))],
        out_specs=[tab, tab, tab, tab],
        out_shape=[tab_b16, tab_b16, tab_f32, tab_f32],
        compiler_params=_cparams(("parallel",)),
        name="topk",
    )(h2t, wqt, keys)


def _gelu(x):
    return 0.5 * x * (1.0 + lax.erf(x * (2.0 ** -0.5)))


def _peer_body(h2t_ref, x1_ref, rk2_ref, e2_ref, cnt_ref, c_ref, u_ref, vt_ref, nw_ref, y_ref,
               acc_scr, w_scr, *, ec):
    e = pl.program_id(1)
    groups = ec // N_KEYS

    @pl.when(e == 0)
    def _():
        acc_scr[...] = jnp.zeros_like(acc_scr)

    zero = jnp.zeros((), BF16)
    act = _gelu(_dot(u_ref[...], h2t_ref[...])).astype(BF16)
    for ii in range(groups):
        i_key = e * groups + ii
        gate = None
        for h in range(PEER_HEADS):
            cnt_row = cnt_ref[h, pl.ds(i_key, 1), :].astype(BF16)
            c_row = c_ref[h, pl.ds(i_key, 1), :].astype(BF16)
            term = jnp.where(rk2_ref[h] < cnt_row, e2_ref[h], zero) * c_row
            gate = term if gate is None else gate + term
        w_scr[ii * N_KEYS:(ii + 1) * N_KEYS, :] = act[ii * N_KEYS:(ii + 1) * N_KEYS] * gate
    acc_scr[...] += _dot(vt_ref[...], w_scr[...])

    @pl.when(e == pl.num_programs(1) - 1)
    def _():
        x2 = x1_ref[...] + acc_scr[...].T
        ms = jnp.mean(x2 * x2, axis=-1, keepdims=True)
        y_ref[...] = x2 * lax.rsqrt(ms + EPS) * nw_ref[...]


def _peer_call(h2t, x1, rk2, e2, cnt, cc, u_b, vt_b, norm_w, tt, ec):
    n = x1.shape[0]
    n_exp = u_b.shape[0]
    tab = pl.BlockSpec((PEER_HEADS, N_KEYS, tt), lambda i, e: (0, 0, i))
    return pl.pallas_call(
        functools.partial(_peer_body, ec=ec),
        grid=(n // tt, n_exp // ec),
        in_specs=[pl.BlockSpec((D_MODEL, tt), lambda i, e: (0, i)),
                  pl.BlockSpec((tt, D_MODEL), lambda i, e: (i, 0)),
                  tab, tab, tab, tab,
                  pl.BlockSpec((ec, D_MODEL), lambda i, e: (e, 0)),
                  pl.BlockSpec((D_MODEL, ec), lambda i, e: (0, e)),
                  pl.BlockSpec((1, D_MODEL), lambda i, e: (0, 0))],
        out_specs=pl.BlockSpec((tt, D_MODEL), lambda i, e: (i, 0)),
        out_shape=jax.ShapeDtypeStruct((n, D_MODEL), F32),
        scratch_shapes=[pltpu.VMEM((D_MODEL, tt), F32), pltpu.VMEM((ec, tt), BF16)],
        compiler_params=_cparams(("parallel", "arbitrary")),
        name="peer",
    )(h2t, x1, rk2, e2, cnt, cc, u_b, vt_b, norm_w)


def _permute_w_in(w):
    splits = [sum(IN_SIZES[:i + 1]) for i in range(len(IN_SIZES) - 1)]
    z, xbc, dt, q, k, v, glr, r, gs, gg = jnp.split(w, splits, axis=1)
    pad = jnp.zeros((D_MODEL, PROJ_COLS - COL_DTG - SSD_HEADS - GLA_GATE_RANK), w.dtype)
    return jnp.concatenate([z, v, r, gs, gg, q, k, xbc, dt, glr, pad], axis=1).astype(BF16)


def _row(v, width=None):
    v = v.reshape(1, -1).astype(F32)
    if width is not None and v.shape[1] < width:
        v = jnp.pad(v, ((0, 0), (0, width - v.shape[1])))
    return v


def _trunk(x, conv0, ssd0, gla0, wts, bg_ssd, q_ssd, bg_gla, q_gla, tm, tt, ec):
    nb, length, _ = x.shape
    n = nb * length
    x2d = x.reshape(n, D_MODEL)
    proj = _proj_call(x2d, wts["norm_mix"], wts["w_in"], tm, 1280)
    p3 = proj.reshape(nb, length, PROJ_COLS)
    ys, conv_new, ssd_new = _ssd_call(p3, conv0, ssd0, wts["conv_w"], wts["conv_b"], wts["dtb"], wts["alog"],
                                      wts["dsk"], wts["ssd_norm"], wts["eexp"], bg_ssd, q_ssd)
    o, gla_new = _gla_call(p3, gla0, wts["w2"], wts["b2"], wts["gla_norm"], bg_gla, q_gla)
    x1, h2t = _merge_call(x2d, ys.reshape(n, SSD_INNER), o.reshape(n, GLA_VAL), proj,
                          wts["wso"], wts["wgo"], wts["wo"], wts["norm_ffn"], tm // 2)
    rk2, e2, cnt, cc = _topk_call(h2t, wts["wqt"], wts["keys"], tt)
    y = _peer_call(h2t, x1, rk2, e2, cnt, cc, wts["u"], wts["vt"], wts["final_norm"], tt, ec)
    return y.reshape(nb, length, D_MODEL), conv_new[None], ssd_new[None], gla_new[None]


def _weights(p):
    assert p["norm_mix_w"].shape[0] == 1, "one layer"
    w2_pad = jnp.zeros((LANES, GLA_KEY), F32).at[SSD_HEADS:SSD_HEADS + GLA_GATE_RANK].set(p["w_gla_gate2"][0])
    head_of_lane = jnp.arange(SSD_INNER) // SSD_HEADDIM
    return {
        "norm_mix": _row(p["norm_mix_w"][0]),
        "w_in": _permute_w_in(p["w_in"][0]),
        "conv_w": p["conv_w"][0], "conv_b": _row(p["conv_b"][0]),
        "dtb": _row(p["dt_bias"][0], LANES), "alog": _row(p["a_log"][0], LANES),
        "dsk": _row(jnp.repeat(p["d_skip"][0], SSD_HEADDIM)),
        "ssd_norm": _row(p["ssd_norm_w"][0]),
        "eexp": (jnp.arange(LANES)[:, None] == head_of_lane[None, :]).astype(F32),
        "w2": w2_pad.astype(BF16), "b2": _row(p["b_gla_gate"][0]), "gla_norm": _row(p["gla_norm_w"][0]),
        "wso": p["w_ssd_out"][0].astype(BF16), "wgo": p["w_gla_out"][0].astype(BF16),
        "wo": p["w_out"][0].astype(BF16),
        "norm_ffn": _row(p["norm_ffn_w"][0]),
        "wqt": p["w_query"][0].T.astype(BF16), "keys": p["sub_keys"][0].astype(BF16),
        "u": p["expert_u"][0].astype(BF16), "vt": p["expert_v"][0].T.astype(BF16),
        "final_norm": _row(p["final_norm_w"]),
    }


def kernel(x_prompt, x_sample, state_conv, state_ssd, state_gla, norm_mix_w, w_in, conv_w, conv_b, dt_bias,
           a_log, d_skip, ssd_norm_w, w_gla_gate2, b_gla_gate, gla_norm_w, w_ssd_out, w_gla_out, w_out,
           norm_ffn_w, w_query, sub_keys, expert_u, expert_v, final_norm_w):
    wts = _weights(dict(
        norm_mix_w=norm_mix_w, w_in=w_in, conv_w=conv_w, conv_b=conv_b, dt_bias=dt_bias, a_log=a_log,
        d_skip=d_skip, ssd_norm_w=ssd_norm_w, w_gla_gate2=w_gla_gate2, b_gla_gate=b_gla_gate,
        gla_norm_w=gla_norm_w, w_ssd_out=w_ssd_out, w_gla_out=w_gla_out, w_out=w_out, norm_ffn_w=norm_ffn_w,
        w_query=w_query, sub_keys=sub_keys, expert_u=expert_u, expert_v=expert_v, final_norm_w=final_norm_w))
    bp = x_prompt.shape[0]
    conv0 = jnp.zeros((bp, CONV_W - 1, CONV_DIM), F32)
    ssd0 = jnp.zeros((bp, SSD_HEADS, SSD_HEADDIM, SSD_STATE), F32)
    gla0 = jnp.zeros((bp, GLA_HEADS, GLA_DK, GLA_DV), F32)
    ls = x_sample.shape[1]
    y_p, conv_p, ssd_p, gla_p = _trunk(x_prompt, conv0, ssd0, gla0, wts,
                                       bg_ssd=1, q_ssd=ROWS, bg_gla=2, q_gla=ROWS // 2, tm=1024, tt=512, ec=1024)
    y_s, conv_s, ssd_s, gla_s = _trunk(x_sample, state_conv[0], state_ssd[0], state_gla[0], wts,
                                       bg_ssd=ROWS // ls, q_ssd=ls, bg_gla=ROWS // ls, q_gla=ls,
                                       tm=1024, tt=512, ec=1024)
    return (y_p, y_s, conv_p, ssd_p, gla_p, conv_s, ssd_s, gla_s)
```

```python
import functools
import math

import jax
import jax.numpy as jnp
from jax import lax
from jax.experimental import pallas as pl
from jax.experimental.pallas import tpu as pltpu

F32 = jnp.float32
BF16 = jnp.bfloat16
HIGHEST = lax.Precision.HIGHEST

D_MODEL = 1024
SSD_HEADS = 16
SSD_HEADDIM = 64
SSD_INNER = 1024
SSD_GROUPS = 2
SSD_STATE = 64
CONV_W = 4
CONV_DIM = 1280
GLA_HEADS = 4
GLA_DK = 128
GLA_DV = 256
GLA_KEY = 512
GLA_VAL = 1024
GLA_GATE_RANK = 16
GLA_GATE_NORM = 16.0
IN_SIZES = (SSD_INNER, CONV_DIM, SSD_HEADS, GLA_KEY, GLA_KEY, GLA_VAL, GLA_GATE_RANK, GLA_VAL, D_MODEL, D_MODEL)
PEER_HEADS = 8
N_KEYS = 128
PEER_HALF = 128
PEER_TOPK = 16
EPS = 1e-6

LANES = 128
SUBLANES = 8
ROWS = 128
VMEM_LIMIT = 56 * 1024 * 1024

COL_Z, COL_V, COL_R, COL_GS, COL_GG = 0, 1024, 2048, 3072, 4096
COL_Q, COL_K, COL_XS, COL_BC, COL_DTG = 5120, 5632, 6144, 7168, 7424
PROJ_COLS = 7680

NT_DIMS = (((1,), (1,)), ((), ()))


def _cparams(sem):
    return pltpu.CompilerParams(dimension_semantics=sem, vmem_limit_bytes=VMEM_LIMIT)


def _dot(a, b):
    return jnp.dot(a, b, preferred_element_type=F32)


def _dot_hi(a, b):
    return jnp.dot(a, b, preferred_element_type=F32, precision=HIGHEST)


def _dot_nt(a, b):
    return lax.dot_general(a, b, NT_DIMS, preferred_element_type=F32)


def _silu(x):
    return x * jax.nn.sigmoid(x)


def _proj_body(x_ref, nw_ref, w_ref, o_ref, h_scr):
    @pl.when(pl.program_id(1) == 0)
    def _():
        x = x_ref[...]
        ms = jnp.mean(x * x, axis=-1, keepdims=True)
        h_scr[...] = (x * lax.rsqrt(ms + EPS) * nw_ref[...]).astype(BF16)

    o_ref[...] = _dot(h_scr[...], w_ref[...])


def _proj_call(x2d, norm_w, w_perm, tm, tn):
    n = x2d.shape[0]
    return pl.pallas_call(
        _proj_body,
        grid=(n // tm, PROJ_COLS // tn),
        in_specs=[
            pl.BlockSpec((tm, D_MODEL), lambda i, j: (i, 0)),
            pl.BlockSpec((1, D_MODEL), lambda i, j: (0, 0)),
            pl.BlockSpec((D_MODEL, tn), lambda i, j: (0, j)),
        ],
        out_specs=pl.BlockSpec((tm, tn), lambda i, j: (i, j)),
        out_shape=jax.ShapeDtypeStruct((n, PROJ_COLS), F32),
        scratch_shapes=[pltpu.VMEM((tm, D_MODEL), BF16)],
        compiler_params=_cparams(("parallel", "arbitrary")),
        name="proj",
    )(x2d, norm_w, w_perm)


def _block_masks(rows, q):
    ri = lax.broadcasted_iota(jnp.int32, (rows, rows), 0)
    ci = lax.broadcasted_iota(jnp.int32, (rows, rows), 1)
    shift = int(math.log2(q))
    same = (ri >> shift) == (ci >> shift)
    causal = same & (ci <= ri)
    return same, causal


def _row_select(rows, width, r0, q):
    rcol = lax.broadcasted_iota(jnp.int32, (rows, 1), 0)
    rowm = (rcol >= r0) & (rcol < r0 + q)
    rfull = lax.broadcasted_iota(jnp.int32, (rows, width), 0)
    sel = jnp.where(rfull == r0 + (q - 1), 1.0, 0.0).astype(F32)
    return rowm, sel


def _ssd_body(xs_ref, bc_ref, dtg_ref, z_ref, cprev_ref, h0_ref, cw_ref, cb_ref, dtb_ref, alog_ref,
              dsk_ref, nw_ref, eexp_ref, ys_ref, cnew_ref, hout_ref,
              xext, hst, c_scr, b_scr, xwt_scr, cdt_scr, yoff_scr, *, bg, q):
    rows = bg * q
    c = pl.program_id(1)
    ngroup = SSD_HEADS // SSD_GROUPS
    gw = ngroup * SSD_HEADDIM

    @pl.when(c == 0)
    def _():
        xext[:, 5:8, :] = cprev_ref[...]
        hst[...] = h0_ref[...]

    xext[:, 8:8 + q, 0:SSD_INNER] = xs_ref[...]
    xext[:, 8:8 + q, SSD_INNER:CONV_DIM] = bc_ref[...]
    acc = cb_ref[...][None]
    for k in range(CONV_W):
        acc = acc + cw_ref[k:k + 1, :][None] * xext[:, 5 + k:5 + k + q, :]
    xc = _silu(acc).reshape(rows, CONV_DIM)
    tail = xext[:, q + 5:q + 8, :]
    cnew_ref[...] = tail
    xext[:, 5:8, :] = tail

    xs = xc[:, :SSD_INNER]
    bm = xc[:, SSD_INNER:SSD_INNER + LANES]
    cm = xc[:, SSD_INNER + LANES:CONV_DIM]

    lane = lax.broadcasted_iota(jnp.int32, (1, LANES), 1)
    hmask = lane < SSD_HEADS
    dt = jnp.where(hmask, jax.nn.softplus(dtg_ref[...].reshape(rows, LANES) + dtb_ref[...]), 0.0)
    a = dt * jnp.where(hmask, -jnp.exp(alog_ref[...]), 0.0)

    same, causal = _block_masks(rows, q)
    acum = _dot_hi(causal.astype(F32), a)
    alast = _dot_hi(same.astype(F32), a)

    eexp = eexp_ref[...]
    acum_x = _dot_hi(acum, eexp)
    alast_x = _dot_hi(alast, eexp)
    xdt = xs * _dot_hi(dt, eexp)
    xw = xdt * jnp.exp(alast_x - acum_x)

    cbs = []
    for g in range(SSD_GROUPS):
        cb = _dot_nt(cm[:, g * SSD_STATE:(g + 1) * SSD_STATE].astype(BF16),
                     bm[:, g * SSD_STATE:(g + 1) * SSD_STATE].astype(BF16))
        cbs.append(jnp.where(causal, cb, 0.0))
    ydiag = []
    for h in range(SSD_HEADS):
        col = jnp.broadcast_to(acum[:, h:h + 1], (rows, rows))
        dec = jnp.exp(jnp.where(causal, col - col.T, 0.0))
        w = (cbs[h // ngroup] * dec).astype(BF16)
        ydiag.append(_dot(w, xdt[:, h * SSD_HEADDIM:(h + 1) * SSD_HEADDIM].astype(BF16)))
    y = jnp.concatenate(ydiag, axis=1)

    c_scr[...] = cm
    b_scr[...] = bm
    cd_x = jnp.exp(alast_x)
    for g in range(SSD_GROUPS):
        xwt_scr[g] = xw[:, g * gw:(g + 1) * gw].T.astype(BF16)
        cdt_scr[g] = cd_x[:, g * gw:(g + 1) * gw].T

    def seq_step(b, carry):
        r0 = pl.multiple_of(b * q, q)
        rowm, sel = _row_select(rows, SSD_STATE, r0, q)
        for g in range(SSD_GROUPS):
            hg = hst[b, g * ngroup:(g + 1) * ngroup].reshape(gw, SSD_STATE)
            cg = c_scr[pl.ds(r0, q), g * SSD_STATE:(g + 1) * SSD_STATE]
            yoff_scr[pl.ds(r0, q), g * gw:(g + 1) * gw] = _dot_nt(cg.astype(BF16), hg.astype(BF16))
            bmask = jnp.where(rowm, b_scr[:, g * SSD_STATE:(g + 1) * SSD_STATE], 0.0).astype(BF16)
            s_new = _dot(xwt_scr[g], bmask)
            cd = _dot_hi(cdt_scr[g], sel)
            hst[b, g * ngroup:(g + 1) * ngroup] = (cd * hg + s_new).reshape(ngroup, SSD_HEADDIM, SSD_STATE)
        return carry

    lax.fori_loop(0, bg, seq_step, 0)

    y = y + yoff_scr[...] * jnp.exp(acum_x) + dsk_ref[...] * xs
    yz = y * _silu(z_ref[...].reshape(rows, SSD_INNER))
    ms = jnp.mean(yz * yz, axis=-1, keepdims=True)
    ys_ref[...] = (yz * lax.rsqrt(ms + EPS) * nw_ref[...]).astype(BF16).reshape(bg, q, SSD_INNER)

    @pl.when(c == pl.num_programs(1) - 1)
    def _():
        hout_ref[...] = hst[...]


def _ssd_call(p3, conv_prev, h0, conv_w, conv_b, dtb, alog, dsk_x, norm_w, eexp, bg, q):
    nb, length, _ = p3.shape
    full = lambda shape: pl.BlockSpec(shape, lambda i, c: (0,) * len(shape))
    rows = bg * q
    gw = SSD_INNER // SSD_GROUPS
    return pl.pallas_call(
        functools.partial(_ssd_body, bg=bg, q=q),
        grid=(nb // bg, length // q),
        in_specs=[
            pl.BlockSpec((bg, q, SSD_INNER), lambda i, c: (i, c, COL_XS // SSD_INNER)),
            pl.BlockSpec((bg, q, 2 * LANES), lambda i, c: (i, c, COL_BC // (2 * LANES))),
            pl.BlockSpec((bg, q, LANES), lambda i, c: (i, c, COL_DTG // LANES)),
            pl.BlockSpec((bg, q, SSD_INNER), lambda i, c: (i, c, COL_Z // SSD_INNER)),
            pl.BlockSpec((bg, CONV_W - 1, CONV_DIM), lambda i, c: (i, 0, 0)),
            pl.BlockSpec((bg, SSD_HEADS, SSD_HEADDIM, SSD_STATE), lambda i, c: (i, 0, 0, 0)),
            full((CONV_W, CONV_DIM)), full((1, CONV_DIM)), full((1, LANES)), full((1, LANES)),
            full((1, SSD_INNER)), full((1, SSD_INNER)), full((LANES, SSD_INNER)),
        ],
        out_specs=[
            pl.BlockSpec((bg, q, SSD_INNER), lambda i, c: (i, c, 0)),
            pl.BlockSpec((bg, CONV_W - 1, CONV_DIM), lambda i, c: (i, 0, 0)),
            pl.BlockSpec((bg, SSD_HEADS, SSD_HEADDIM, SSD_STATE), lambda i, c: (i, 0, 0, 0)),
        ],
        out_shape=[
            jax.ShapeDtypeStruct((nb, length, SSD_INNER), BF16),
            jax.ShapeDtypeStruct((nb, CONV_W - 1, CONV_DIM), F32),
            jax.ShapeDtypeStruct((nb, SSD_HEADS, SSD_HEADDIM, SSD_STATE), F32),
        ],
        scratch_shapes=[
            pltpu.VMEM((bg, q + 8, CONV_DIM), F32),
            pltpu.VMEM((bg, SSD_HEADS, SSD_HEADDIM, SSD_STATE), F32),
            pltpu.VMEM((rows, LANES), F32),
            pltpu.VMEM((rows, LANES), F32),
            pltpu.VMEM((SSD_GROUPS, gw, rows), BF16),
            pltpu.VMEM((SSD_GROUPS, gw, rows), F32),
            pltpu.VMEM((rows, SSD_INNER), F32),
        ],
        compiler_params=_cparams(("parallel", "arbitrary")),
        name="ssd",
    )(p3, p3, p3, p3, conv_prev, h0, conv_w, conv_b, dtb, alog, dsk_x, norm_w, eexp)


def _gla_body(q_ref, k_ref, v_ref, r_ref, dtg_ref, s0_ref, w2_ref, b2_ref, nw_ref, o_ref, sout_ref,
              sst, qe_scr, kdt_scr, v_scr, cdt_scr, oint_scr, *, bg, q):
    rows = bg * q
    c = pl.program_id(1)

    @pl.when(c == 0)
    def _():
        sst[...] = s0_ref[...]

    qv = q_ref[...].reshape(rows, GLA_KEY)
    kv = k_ref[...].reshape(rows, GLA_KEY)
    vv = v_ref[...].reshape(rows, GLA_VAL)
    glr = dtg_ref[...].reshape(rows, LANES)
    g = jax.nn.log_sigmoid(_dot(glr.astype(BF16), w2_ref[...]) + b2_ref[...]) * (1.0 / GLA_GATE_NORM)

    same, causal = _block_masks(rows, q)
    gc = _dot_hi(causal.astype(F32), g)
    gl = _dot_hi(same.astype(F32), g)
    qe = qv * (GLA_DK ** -0.5) * jnp.exp(gc)
    ke = kv * jnp.exp(-gc)
    kd = kv * jnp.exp(gl - gc)
    cdk = jnp.exp(gl)

    qe_b = qe.astype(BF16)
    ke_b = ke.astype(BF16)
    v_b = vv.astype(BF16)
    o_intra = []
    for h in range(GLA_HEADS):
        ks = slice(h * GLA_DK, (h + 1) * GLA_DK)
        att = jnp.where(causal, _dot_nt(qe_b[:, ks], ke_b[:, ks]), 0.0)
        o_intra.append(_dot(att.astype(BF16), v_b[:, h * GLA_DV:(h + 1) * GLA_DV]))
        kdt_scr[h] = kd[:, ks].T.astype(BF16)
        cdt_scr[h] = cdk[:, ks].T
    qe_scr[...] = qe
    v_scr[...] = vv

    def seq_step(b, carry):
        r0 = pl.multiple_of(b * q, q)
        rowm, sel = _row_select(rows, GLA_DV, r0, q)
        for h in range(GLA_HEADS):
            s_old = sst[b, h]
            qeb = qe_scr[pl.ds(r0, q), h * GLA_DK:(h + 1) * GLA_DK]
            oint_scr[pl.ds(r0, q), h * GLA_DV:(h + 1) * GLA_DV] = _dot(qeb.astype(BF16), s_old.astype(BF16))
            vm = jnp.where(rowm, v_scr[:, h * GLA_DV:(h + 1) * GLA_DV], 0.0).astype(BF16)
            s_new = _dot(kdt_scr[h], vm)
            cd = _dot_hi(cdt_scr[h], sel)
            sst[b, h] = cd * s_old + s_new
        return carry

    lax.fori_loop(0, bg, seq_step, 0)

    rv = r_ref[...].reshape(rows, GLA_VAL)
    outs = []
    for h in range(GLA_HEADS):
        vs = slice(h * GLA_DV, (h + 1) * GLA_DV)
        o = o_intra[h] + oint_scr[:, vs]
        ms = jnp.mean(o * o, axis=-1, keepdims=True)
        outs.append(o * lax.rsqrt(ms + EPS) * nw_ref[...])
    o_all = jnp.concatenate(outs, axis=1) * _silu(rv)
    o_ref[...] = o_all.astype(BF16).reshape(bg, q, GLA_VAL)

    @pl.when(c == pl.num_programs(1) - 1)
    def _():
        sout_ref[...] = sst[...]


def _gla_call(p3, s0, w2_pad, b2, norm_w, bg, q):
    nb, length, _ = p3.shape
    full = lambda shape: pl.BlockSpec(shape, lambda i, c: (0,) * len(shape))
    rows = bg * q
    state_block = (bg, GLA_HEADS, GLA_DK, GLA_DV)
    return pl.pallas_call(
        functools.partial(_gla_body, bg=bg, q=q),
        grid=(nb // bg, length // q),
        in_specs=[
            pl.BlockSpec((bg, q, GLA_KEY), lambda i, c: (i, c, COL_Q // GLA_KEY)),
            pl.BlockSpec((bg, q, GLA_KEY), lambda i, c: (i, c, COL_K // GLA_KEY)),
            pl.BlockSpec((bg, q, GLA_VAL), lambda i, c: (i, c, COL_V // GLA_VAL)),
            pl.BlockSpec((bg, q, GLA_VAL), lambda i, c: (i, c, COL_R // GLA_VAL)),
            pl.BlockSpec((bg, q, LANES), lambda i, c: (i, c, COL_DTG // LANES)),
            pl.BlockSpec(state_block, lambda i, c: (i, 0, 0, 0)),
            full((LANES, GLA_KEY)), full((1, GLA_KEY)), full((1, GLA_DV)),
        ],
        out_specs=[
            pl.BlockSpec((bg, q, GLA_VAL), lambda i, c: (i, c, 0)),
            pl.BlockSpec(state_block, lambda i, c: (i, 0, 0, 0)),
        ],
        out_shape=[
            jax.ShapeDtypeStruct((nb, length, GLA_VAL), BF16),
            jax.ShapeDtypeStruct((nb, GLA_HEADS, GLA_DK, GLA_DV), F32),
        ],
        scratch_shapes=[
            pltpu.VMEM(state_block, F32),
            pltpu.VMEM((rows, GLA_KEY), F32),
            pltpu.VMEM((GLA_HEADS, GLA_DK, rows), BF16),
            pltpu.VMEM((rows, GLA_VAL), F32),
            pltpu.VMEM((GLA_HEADS, GLA_DK, rows), F32),
            pltpu.VMEM((rows, GLA_VAL), F32),
        ],
        compiler_params=_cparams(("parallel", "arbitrary")),
        name="gla",
    )(p3, p3, p3, p3, p3, s0, w2_pad, b2, norm_w)


def _merge_body(x_ref, ys_ref, o_ref, gs_ref, gg_ref, wso_ref, wgo_ref, wo_ref, nw_ref, x1_ref, h2t_ref):
    mix = (jax.nn.sigmoid(gs_ref[...]) * _dot(ys_ref[...], wso_ref[...])
           + jax.nn.sigmoid(gg_ref[...]) * _dot(o_ref[...], wgo_ref[...]))
    x1 = x_ref[...] + _dot(mix.astype(BF16), wo_ref[...])
    x1_ref[...] = x1
    ms = jnp.mean(x1 * x1, axis=-1, keepdims=True)
    h2 = x1 * lax.rsqrt(ms + EPS) * nw_ref[...]
    h2t_ref[...] = h2.T.astype(BF16)


def _merge_call(x2d, ys, o, proj, wso, wgo, wo, norm_w, tm):
    n = x2d.shape[0]
    tok = lambda col: pl.BlockSpec((tm, D_MODEL), lambda i: (i, col))
    wfull = pl.BlockSpec((D_MODEL, D_MODEL), lambda i: (0, 0))
    return pl.pallas_call(
        _merge_body,
        grid=(n // tm,),
        in_specs=[tok(0), tok(0), tok(0), tok(COL_GS // D_MODEL), tok(COL_GG // D_MODEL),
                  wfull, wfull, wfull, pl.BlockSpec((1, D_MODEL), lambda i: (0, 0))],
        out_specs=[pl.BlockSpec((tm, D_MODEL), lambda i: (i, 0)),
                   pl.BlockSpec((D_MODEL, tm), lambda i: (0, i))],
        out_shape=[jax.ShapeDtypeStruct((n, D_MODEL), F32),
                   jax.ShapeDtypeStruct((D_MODEL, n), BF16)],
        compiler_params=_cparams(("parallel",)),
        name="merge",
    )(x2d, ys, o, proj, proj, wso, wgo, wo, norm_w)


NEG_INF = float("-inf")
PAIR_ROWS = ((1, 8), (2, 5), (3, 4), (4, 3), (5, 2), (6, 2), (7, 2))
N_PAIR_PAD = 30


def _count_rows(mask):
    return jnp.sum(jnp.where(mask, 1.0, 0.0), axis=0, keepdims=True)


def _top16_fast(s, want_rank):
    row = lax.broadcasted_iota(jnp.int32, (PEER_TOPK, s.shape[1]), 0)
    vals = jnp.zeros((PEER_TOPK, s.shape[1]), F32)
    rank = jnp.full(s.shape, float(PEER_TOPK), F32) if want_rank else None
    cur = s
    for it in range(PEER_TOPK):
        m = jnp.max(cur, axis=0, keepdims=True)
        vals = jnp.where(row == it, m, vals)
        eq = cur == m
        if want_rank:
            rank = jnp.where(eq, float(it), rank)
        cur = jnp.where(eq, NEG_INF, cur)
    return vals, rank, _count_rows(cur == NEG_INF)


def _top16_exact(s):
    row = lax.broadcasted_iota(jnp.int32, (PEER_TOPK, s.shape[1]), 0)
    idx = lax.broadcasted_iota(jnp.int32, s.shape, 0).astype(F32)
    vals = jnp.zeros((PEER_TOPK, s.shape[1]), F32)
    rank = jnp.full(s.shape, float(PEER_TOPK), F32)
    cur = s
    for it in range(PEER_TOPK):
        m = jnp.max(cur, axis=0, keepdims=True)
        vals = jnp.where(row == it, m, vals)
        hit = idx == jnp.min(jnp.where(cur == m, idx, float(s.shape[0])), axis=0, keepdims=True)
        rank = jnp.where(hit, float(it), rank)
        cur = jnp.where(hit, NEG_INF, cur)
    return vals, rank


def _head_fast(s1, s2):
    tt = s1.shape[1]
    row8 = lax.broadcasted_iota(jnp.int32, (SUBLANES, tt), 0)
    v1, _, gone1 = _top16_fast(s1, False)
    v2, rank2, gone2 = _top16_fast(s2, True)
    cands = [v1[0:1] + v2]
    v2lo = v2[0:SUBLANES]
    for a, nb in PAIR_ROWS:
        cands.append(jnp.where(row8 < nb, v1[a:a + 1] + v2lo, NEG_INF))
    cands.append(v1[SUBLANES:PEER_TOPK] + v2[0:1])
    cur = jnp.concatenate(cands, axis=0)
    top = v1[0:1] + v2[0:1]
    zsum = jnp.zeros((1, tt), F32)
    tau = top
    for _ in range(PEER_TOPK):
        tau = jnp.max(cur, axis=0, keepdims=True)
        zsum = zsum + jnp.exp(tau - top)
        cur = jnp.where(cur == tau, NEG_INF, cur)
    gone3 = _count_rows(cur == NEG_INF) - float(N_PAIR_PAD)
    cnt16 = jnp.zeros((PEER_TOPK, tt), F32)
    for b in range(PEER_TOPK):
        cnt16 = cnt16 + jnp.where(v1 + v2[b:b + 1] >= tau, 1.0, 0.0)
    cnt = jnp.zeros((N_KEYS, tt), F32)
    for a in range(PEER_TOPK):
        cnt = jnp.where(s1 == v1[a:a + 1], cnt16[a:a + 1], cnt)
    tables = (rank2, jnp.exp(s2 - v2[0:1]), cnt, jnp.exp(s1 - v1[0:1]) / zsum)
    return tables, jnp.maximum(jnp.maximum(gone1, gone2), gone3)


def _head_exact(s1, s2):
    tt = s1.shape[1]
    v1, rank1 = _top16_exact(s1)
    v2, rank2 = _top16_exact(s2)
    sums = jnp.concatenate([v1[a:a + 1] + v2 for a in range(PEER_TOPK)], axis=0)
    n_pairs = PEER_TOPK * PEER_TOPK
    flat = lax.broadcasted_iota(jnp.int32, (n_pairs, tt), 0).astype(F32)
    top = v1[0:1] + v2[0:1]
    zsum = jnp.zeros((1, tt), F32)
    cur = sums
    tau = top
    last = jnp.zeros((1, tt), F32)
    for _ in range(PEER_TOPK):
        tau = jnp.max(cur, axis=0, keepdims=True)
        last = jnp.min(jnp.where(cur == tau, flat, float(n_pairs)), axis=0, keepdims=True)
        zsum = zsum + jnp.exp(tau - top)
        cur = jnp.where(flat == last, NEG_INF, cur)
    taken = jnp.where(sums > tau, 1.0, jnp.where(sums == tau, jnp.where(flat <= last, 1.0, 0.0), 0.0))
    cnt = jnp.zeros((N_KEYS, tt), F32)
    for a in range(PEER_TOPK):
        cnt_a = jnp.sum(taken[a * PEER_TOPK:(a + 1) * PEER_TOPK], axis=0, keepdims=True)
        cnt = jnp.where(rank1 == float(a), cnt_a, cnt)
    return rank2, jnp.exp(s2 - v2[0:1]), cnt, jnp.exp(s1 - v1[0:1]) / zsum


def _topk_body(h2t_ref, wqt_ref, keys_ref, rk2_ref, e2_ref, cnt_ref, c_ref, s_scr, tie_scr):
    qt = _dot(wqt_ref[...], h2t_ref[...])

    def store(h, tables):
        rank2, e2, cnt, c = tables
        rk2_ref[h] = rank2.astype(BF16)
        e2_ref[h] = e2.astype(BF16)
        cnt_ref[h] = cnt
        c_ref[h] = c

    for h in range(PEER_HEADS):
        base = h * 2 * PEER_HALF
        s1 = _dot(keys_ref[h, 0], qt[base:base + PEER_HALF].astype(BF16))
        s2 = _dot(keys_ref[h, 1], qt[base + PEER_HALF:base + 2 * PEER_HALF].astype(BF16))
        tables, gone = _head_fast(s1, s2)
        store(h, tables)
        s_scr[2 * h] = s1
        s_scr[2 * h + 1] = s2
        tie_scr[h] = (jnp.max(gone) > float(PEER_TOPK)).astype(jnp.int32)

    def redo(h, carry):
        @pl.when(tie_scr[h] > 0)
        def _():
            store(h, _head_exact(s_scr[2 * h], s_scr[2 * h + 1]))
        return carry

    lax.fori_loop(0, PEER_HEADS, redo, 0)


def _topk_call(h2t, wqt, keys, tt):
    n = h2t.shape[1]
    tab = pl.BlockSpec((PEER_HEADS, N_KEYS, tt), lambda i: (0, 0, i))
    tab_f32 = jax.ShapeDtypeStruct((PEER_HEADS, N_KEYS, n), F32)
    tab_b16 = jax.ShapeDtypeStruct((PEER_HEADS, N_KEYS, n), BF16)
    return pl.pallas_call(
        _topk_body,
        grid=(n // tt,),
        in_specs=[pl.BlockSpec((D_MODEL, tt), lambda i: (0, i)),
                  pl.BlockSpec(wqt.shape, lambda i: (0, 0)),
                  pl.BlockSpec(keys.shape, lambda i: (0, 0, 0, 0))],
        out_specs=[tab, tab, tab, tab],
        out_shape=[tab_b16, tab_b16, tab_f32, tab_f32],
        scratch_shapes=[pltpu.VMEM((2 * PEER_HEADS, N_KEYS, tt), F32), pltpu.SMEM((PEER_HEADS,), jnp.int32)],
        compiler_params=_cparams(("parallel",)),
        name="topk",
    )(h2t, wqt, keys)


def _gelu(x):
    x = x.astype(BF16)
    return (0.5 * x) * (1.0 + lax.erf(x * (2.0 ** -0.5)))


def _peer_gate_mul(act_scr, w_scr, rk2_ref, e2_ref, cnt_ref, c_ref, chunk, groups):
    tt = act_scr.shape[1]
    zero = jnp.zeros((), BF16)
    key0 = pl.multiple_of(chunk * groups, groups)
    for ii in range(groups):
        gate = None
        for h in range(PEER_HEADS):
            cnt_row = cnt_ref[h, pl.ds(key0, groups), :][ii:ii + 1]
            c_row = c_ref[h, pl.ds(key0, groups), :][ii:ii + 1]
            cnt_t = jnp.broadcast_to(cnt_row, (N_KEYS, tt)).astype(BF16)
            c_t = jnp.broadcast_to(c_row, (N_KEYS, tt)).astype(BF16)
            term = jnp.where(rk2_ref[h] < cnt_t, e2_ref[h], zero) * c_t
            gate = term if gate is None else gate + term
        rows = slice(ii * N_KEYS, (ii + 1) * N_KEYS)
        w_scr[rows, :] = act_scr[rows, :] * gate


def _peer_body(h2t_ref, x1_ref, rk2_ref, e2_ref, cnt_ref, c_ref, u_ref, vt_ref, nw_ref, y_ref,
               acc_scr, act_scr, w_scr, *, ec):
    e = pl.program_id(1)
    groups = ec // N_KEYS

    @pl.when(e == 0)
    def _():
        acc_scr[...] = jnp.zeros_like(acc_scr)

    act_scr[...] = _gelu(_dot(u_ref[...], h2t_ref[...]))
    _peer_gate_mul(act_scr, w_scr, rk2_ref, e2_ref, cnt_ref, c_ref, e, groups)
    acc_scr[...] += _dot(vt_ref[...], w_scr[...])

    @pl.when(e == pl.num_programs(1) - 1)
    def _():
        x2 = x1_ref[...] + acc_scr[...].T
        ms = jnp.mean(x2 * x2, axis=-1, keepdims=True)
        y_ref[...] = x2 * lax.rsqrt(ms + EPS) * nw_ref[...]


def _peer_call(h2t, x1, rk2, e2, cnt, cc, u_b, vt_b, norm_w, tt, ec):
    n = x1.shape[0]
    assert ec // N_KEYS == SUBLANES, "one expert chunk = one aligned sublane tile of first-half keys"
    tab = pl.BlockSpec((PEER_HEADS, N_KEYS, tt), lambda i, e: (0, 0, i))
    chunk_shape = pltpu.VMEM((ec, tt), BF16)
    return pl.pallas_call(
        functools.partial(_peer_body, ec=ec),
        grid=(n // tt, u_b.shape[0] // ec),
        in_specs=[pl.BlockSpec((D_MODEL, tt), lambda i, e: (0, i)),
                  pl.BlockSpec((tt, D_MODEL), lambda i, e: (i, 0)),
                  tab, tab, tab, tab,
                  pl.BlockSpec((ec, D_MODEL), lambda i, e: (e, 0)),
                  pl.BlockSpec((D_MODEL, ec), lambda i, e: (0, e)),
                  pl.BlockSpec((1, D_MODEL), lambda i, e: (0, 0))],
        out_specs=pl.BlockSpec((tt, D_MODEL), lambda i, e: (i, 0)),
        out_shape=jax.ShapeDtypeStruct((n, D_MODEL), F32),
        scratch_shapes=[pltpu.VMEM((D_MODEL, tt), F32), chunk_shape, chunk_shape],
        compiler_params=_cparams(("parallel", "arbitrary")),
        name="peer",
    )(h2t, x1, rk2, e2, cnt, cc, u_b, vt_b, norm_w)


def _permute_w_in(w):
    splits = [sum(IN_SIZES[:i + 1]) for i in range(len(IN_SIZES) - 1)]
    z, xbc, dt, q, k, v, glr, r, gs, gg = jnp.split(w, splits, axis=1)
    pad = jnp.zeros((D_MODEL, PROJ_COLS - COL_DTG - SSD_HEADS - GLA_GATE_RANK), w.dtype)
    return jnp.concatenate([z, v, r, gs, gg, q, k, xbc, dt, glr, pad], axis=1).astype(BF16)


def _row(v, width=None):
    v = v.reshape(1, -1).astype(F32)
    if width is not None and v.shape[1] < width:
        v = jnp.pad(v, ((0, 0), (0, width - v.shape[1])))
    return v


def _trunk(x, conv0, ssd0, gla0, wts, bg_ssd, q_ssd, bg_gla, q_gla, tm, tt, ec):
    nb, length, _ = x.shape
    n = nb * length
    x2d = x.reshape(n, D_MODEL)
    proj = _proj_call(x2d, wts["norm_mix"], wts["w_in"], tm, 1280)
    p3 = proj.reshape(nb, length, PROJ_COLS)
    ys, conv_new, ssd_new = _ssd_call(p3, conv0, ssd0, wts["conv_w"], wts["conv_b"], wts["dtb"], wts["alog"],
                                      wts["dsk"], wts["ssd_norm"], wts["eexp"], bg_ssd, q_ssd)
    o, gla_new = _gla_call(p3, gla0, wts["w2"], wts["b2"], wts["gla_norm"], bg_gla, q_gla)
    x1, h2t = _merge_call(x2d, ys.reshape(n, SSD_INNER), o.reshape(n, GLA_VAL), proj,
                          wts["wso"], wts["wgo"], wts["wo"], wts["norm_ffn"], tm // 2)
    rk2, e2, cnt, cc = _topk_call(h2t, wts["wqt"], wts["keys"], tt)
    y = _peer_call(h2t, x1, rk2, e2, cnt, cc, wts["u"], wts["vt"], wts["final_norm"], tt, ec)
    return y.reshape(nb, length, D_MODEL), conv_new[None], ssd_new[None], gla_new[None]


def _weights(p):
    assert p["norm_mix_w"].shape[0] == 1, "one layer"
    w2_pad = jnp.zeros((LANES, GLA_KEY), F32).at[SSD_HEADS:SSD_HEADS + GLA_GATE_RANK].set(p["w_gla_gate2"][0])
    head_of_lane = jnp.arange(SSD_INNER) // SSD_HEADDIM
    return {
        "norm_mix": _row(p["norm_mix_w"][0]),
        "w_in": _permute_w_in(p["w_in"][0]),
        "conv_w": p["conv_w"][0], "conv_b": _row(p["conv_b"][0]),
        "dtb": _row(p["dt_bias"][0], LANES), "alog": _row(p["a_log"][0], LANES),
        "dsk": _row(jnp.repeat(p["d_skip"][0], SSD_HEADDIM)),
        "ssd_norm": _row(p["ssd_norm_w"][0]),
        "eexp": (jnp.arange(LANES)[:, None] == head_of_lane[None, :]).astype(F32),
        "w2": w2_pad.astype(BF16), "b2": _row(p["b_gla_gate"][0]), "gla_norm": _row(p["gla_norm_w"][0]),
        "wso": p["w_ssd_out"][0].astype(BF16), "wgo": p["w_gla_out"][0].astype(BF16),
        "wo": p["w_out"][0].astype(BF16),
        "norm_ffn": _row(p["norm_ffn_w"][0]),
        "wqt": p["w_query"][0].T.astype(BF16), "keys": p["sub_keys"][0].astype(BF16),
        "u": p["expert_u"][0].astype(BF16), "vt": p["expert_v"][0].T.astype(BF16),
        "final_norm": _row(p["final_norm_w"]),
    }


def kernel(x_prompt, x_sample, state_conv, state_ssd, state_gla, norm_mix_w, w_in, conv_w, conv_b, dt_bias,
           a_log, d_skip, ssd_norm_w, w_gla_gate2, b_gla_gate, gla_norm_w, w_ssd_out, w_gla_out, w_out,
           norm_ffn_w, w_query, sub_keys, expert_u, expert_v, final_norm_w):
    wts = _weights(dict(
        norm_mix_w=norm_mix_w, w_in=w_in, conv_w=conv_w, conv_b=conv_b, dt_bias=dt_bias, a_log=a_log,
        d_skip=d_skip, ssd_norm_w=ssd_norm_w, w_gla_gate2=w_gla_gate2, b_gla_gate=b_gla_gate,
        gla_norm_w=gla_norm_w, w_ssd_out=w_ssd_out, w_gla_out=w_gla_out, w_out=w_out, norm_ffn_w=norm_ffn_w,
        w_query=w_query, sub_keys=sub_keys, expert_u=expert_u, expert_v=expert_v, final_norm_w=final_norm_w))
    bp = x_prompt.shape[0]
    conv0 = jnp.zeros((bp, CONV_W - 1, CONV_DIM), F32)
    ssd0 = jnp.zeros((bp, SSD_HEADS, SSD_HEADDIM, SSD_STATE), F32)
    gla0 = jnp.zeros((bp, GLA_HEADS, GLA_DK, GLA_DV), F32)
    ls = x_sample.shape[1]
    y_p, conv_p, ssd_p, gla_p = _trunk(x_prompt, conv0, ssd0, gla0, wts,
                                       bg_ssd=1, q_ssd=ROWS, bg_gla=2, q_gla=ROWS // 2, tm=1024, tt=512, ec=1024)
    y_s, conv_s, ssd_s, gla_s = _trunk(x_sample, state_conv[0], state_ssd[0], state_gla[0], wts,
                                       bg_ssd=ROWS // ls, q_ssd=ls, bg_gla=ROWS // ls, q_gla=ls,
                                       tm=1024, tt=512, ec=1024)
    return (y_p, y_s, conv_p, ssd_p, gla_p, conv_s, ssd_s, gla_s)
```

```python
import functools
import math

import jax
import jax.numpy as jnp
from jax import lax
from jax.experimental import pallas as pl
from jax.experimental.pallas import tpu as pltpu

F32 = jnp.float32
BF16 = jnp.bfloat16
HIGHEST = lax.Precision.HIGHEST

D_MODEL = 1024
SSD_HEADS = 16
SSD_HEADDIM = 64
SSD_INNER = 1024
SSD_GROUPS = 2
SSD_STATE = 64
CONV_W = 4
CONV_DIM = 1280
GLA_HEADS = 4
GLA_DK = 128
GLA_DV = 256
GLA_KEY = 512
GLA_VAL = 1024
GLA_GATE_RANK = 16
GLA_GATE_NORM = 16.0
IN_SIZES = (SSD_INNER, CONV_DIM, SSD_HEADS, GLA_KEY, GLA_KEY, GLA_VAL, GLA_GATE_RANK, GLA_VAL, D_MODEL, D_MODEL)
PEER_HEADS = 8
N_KEYS = 128
PEER_HALF = 128
PEER_TOPK = 16
EPS = 1e-6

LANES = 128
SUBLANES = 8
ROWS = 128
VMEM_LIMIT = 56 * 1024 * 1024

COL_Z, COL_V, COL_R, COL_GS, COL_GG = 0, 1024, 2048, 3072, 4096
COL_Q, COL_K, COL_XS, COL_BC, COL_DTG = 5120, 5632, 6144, 7168, 7424
PROJ_COLS = 7680

NT_DIMS = (((1,), (1,)), ((), ()))


def _cparams(sem):
    return pltpu.CompilerParams(dimension_semantics=sem, vmem_limit_bytes=VMEM_LIMIT)


def _dot(a, b):
    return jnp.dot(a, b, preferred_element_type=F32)


def _dot_hi(a, b):
    return jnp.dot(a, b, preferred_element_type=F32, precision=HIGHEST)


def _dot_nt(a, b):
    return lax.dot_general(a, b, NT_DIMS, preferred_element_type=F32)


def _silu(x):
    return x * jax.nn.sigmoid(x)


def _proj_body(x_ref, nw_ref, w_ref, o_ref, h_scr):
    @pl.when(pl.program_id(1) == 0)
    def _():
        x = x_ref[...]
        ms = jnp.mean(x * x, axis=-1, keepdims=True)
        h_scr[...] = (x * lax.rsqrt(ms + EPS) * nw_ref[...]).astype(BF16)

    o_ref[...] = _dot(h_scr[...], w_ref[...])


def _proj_call(x2d, norm_w, w_perm, tm, tn):
    n = x2d.shape[0]
    return pl.pallas_call(
        _proj_body,
        grid=(n // tm, PROJ_COLS // tn),
        in_specs=[
            pl.BlockSpec((tm, D_MODEL), lambda i, j: (i, 0)),
            pl.BlockSpec((1, D_MODEL), lambda i, j: (0, 0)),
            pl.BlockSpec((D_MODEL, tn), lambda i, j: (0, j)),
        ],
        out_specs=pl.BlockSpec((tm, tn), lambda i, j: (i, j)),
        out_shape=jax.ShapeDtypeStruct((n, PROJ_COLS), F32),
        scratch_shapes=[pltpu.VMEM((tm, D_MODEL), BF16)],
        compiler_params=_cparams(("parallel", "arbitrary")),
        name="proj",
    )(x2d, norm_w, w_perm)


def _block_masks(rows, q):
    ri = lax.broadcasted_iota(jnp.int32, (rows, rows), 0)
    ci = lax.broadcasted_iota(jnp.int32, (rows, rows), 1)
    shift = int(math.log2(q))
    same = (ri >> shift) == (ci >> shift)
    causal = same & (ci <= ri)
    return same, causal


def _row_select(rows, width, r0, q):
    rcol = lax.broadcasted_iota(jnp.int32, (rows, 1), 0)
    rowm = (rcol >= r0) & (rcol < r0 + q)
    rfull = lax.broadcasted_iota(jnp.int32, (rows, width), 0)
    sel = jnp.where(rfull == r0 + (q - 1), 1.0, 0.0).astype(F32)
    return rowm, sel


def _ssd_body(xs_ref, bc_ref, dtg_ref, z_ref, cprev_ref, h0_ref, cw_ref, cb_ref, dtb_ref, alog_ref,
              dsk_ref, nw_ref, eexp_ref, ys_ref, cnew_ref, hout_ref,
              xext, hst, c_scr, b_scr, xwt_scr, cdt_scr, yoff_scr, *, bg, q):
    rows = bg * q
    c = pl.program_id(1)
    ngroup = SSD_HEADS // SSD_GROUPS
    gw = ngroup * SSD_HEADDIM

    @pl.when(c == 0)
    def _():
        xext[:, 5:8, :] = cprev_ref[...]
        hst[...] = h0_ref[...]

    xext[:, 8:8 + q, 0:SSD_INNER] = xs_ref[...]
    xext[:, 8:8 + q, SSD_INNER:CONV_DIM] = bc_ref[...]
    acc = cb_ref[...][None]
    for k in range(CONV_W):
        acc = acc + cw_ref[k:k + 1, :][None] * xext[:, 5 + k:5 + k + q, :]
    xc = _silu(acc).reshape(rows, CONV_DIM)
    tail = xext[:, q + 5:q + 8, :]
    cnew_ref[...] = tail
    xext[:, 5:8, :] = tail

    xs = xc[:, :SSD_INNER]
    bm = xc[:, SSD_INNER:SSD_INNER + LANES]
    cm = xc[:, SSD_INNER + LANES:CONV_DIM]

    lane = lax.broadcasted_iota(jnp.int32, (1, LANES), 1)
    hmask = lane < SSD_HEADS
    dt = jnp.where(hmask, jax.nn.softplus(dtg_ref[...].reshape(rows, LANES) + dtb_ref[...]), 0.0)
    a = dt * jnp.where(hmask, -jnp.exp(alog_ref[...]), 0.0)

    same, causal = _block_masks(rows, q)
    acum = _dot_hi(causal.astype(F32), a)
    alast = _dot_hi(same.astype(F32), a)

    eexp = eexp_ref[...]
    acum_x = _dot_hi(acum, eexp)
    alast_x = _dot_hi(alast, eexp)
    xdt = xs * _dot_hi(dt, eexp)
    xw = xdt * jnp.exp(alast_x - acum_x)

    cbs = []
    for g in range(SSD_GROUPS):
        cb = _dot_nt(cm[:, g * SSD_STATE:(g + 1) * SSD_STATE].astype(BF16),
                     bm[:, g * SSD_STATE:(g + 1) * SSD_STATE].astype(BF16))
        cbs.append(jnp.where(causal, cb, 0.0))
    ydiag = []
    for h in range(SSD_HEADS):
        col = jnp.broadcast_to(acum[:, h:h + 1], (rows, rows))
        dec = jnp.exp(jnp.where(causal, col - col.T, 0.0))
        w = (cbs[h // ngroup] * dec).astype(BF16)
        ydiag.append(_dot(w, xdt[:, h * SSD_HEADDIM:(h + 1) * SSD_HEADDIM].astype(BF16)))
    y = jnp.concatenate(ydiag, axis=1)

    c_scr[...] = cm
    b_scr[...] = bm
    cd_x = jnp.exp(alast_x)
    for g in range(SSD_GROUPS):
        xwt_scr[g] = xw[:, g * gw:(g + 1) * gw].T.astype(BF16)
        cdt_scr[g] = cd_x[:, g * gw:(g + 1) * gw].T

    def seq_step(b, carry):
        r0 = pl.multiple_of(b * q, q)
        rowm, sel = _row_select(rows, SSD_STATE, r0, q)
        for g in range(SSD_GROUPS):
            hg = hst[b, g * ngroup:(g + 1) * ngroup].reshape(gw, SSD_STATE)
            cg = c_scr[pl.ds(r0, q), g * SSD_STATE:(g + 1) * SSD_STATE]
            yoff_scr[pl.ds(r0, q), g * gw:(g + 1) * gw] = _dot_nt(cg.astype(BF16), hg.astype(BF16))
            bmask = jnp.where(rowm, b_scr[:, g * SSD_STATE:(g + 1) * SSD_STATE], 0.0).astype(BF16)
            s_new = _dot(xwt_scr[g], bmask)
            cd = _dot_hi(cdt_scr[g], sel)
            hst[b, g * ngroup:(g + 1) * ngroup] = (cd * hg + s_new).reshape(ngroup, SSD_HEADDIM, SSD_STATE)
        return carry

    lax.fori_loop(0, bg, seq_step, 0)

    y = y + yoff_scr[...] * jnp.exp(acum_x) + dsk_ref[...] * xs
    yz = y * _silu(z_ref[...].reshape(rows, SSD_INNER))
    ms = jnp.mean(yz * yz, axis=-1, keepdims=True)
    ys_ref[...] = (yz * lax.rsqrt(ms + EPS) * nw_ref[...]).astype(BF16).reshape(bg, q, SSD_INNER)

    @pl.when(c == pl.num_programs(1) - 1)
    def _():
        hout_ref[...] = hst[...]


def _ssd_call(p3, conv_prev, h0, conv_w, conv_b, dtb, alog, dsk_x, norm_w, eexp, bg, q):
    nb, length, _ = p3.shape
    full = lambda shape: pl.BlockSpec(shape, lambda i, c: (0,) * len(shape))
    rows = bg * q
    gw = SSD_INNER // SSD_GROUPS
    return pl.pallas_call(
        functools.partial(_ssd_body, bg=bg, q=q),
        grid=(nb // bg, length // q),
        in_specs=[
            pl.BlockSpec((bg, q, SSD_INNER), lambda i, c: (i, c, COL_XS // SSD_INNER)),
            pl.BlockSpec((bg, q, 2 * LANES), lambda i, c: (i, c, COL_BC // (2 * LANES))),
            pl.BlockSpec((bg, q, LANES), lambda i, c: (i, c, COL_DTG // LANES)),
            pl.BlockSpec((bg, q, SSD_INNER), lambda i, c: (i, c, COL_Z // SSD_INNER)),
            pl.BlockSpec((bg, CONV_W - 1, CONV_DIM), lambda i, c: (i, 0, 0)),
            pl.BlockSpec((bg, SSD_HEADS, SSD_HEADDIM, SSD_STATE), lambda i, c: (i, 0, 0, 0)),
            full((CONV_W, CONV_DIM)), full((1, CONV_DIM)), full((1, LANES)), full((1, LANES)),
            full((1, SSD_INNER)), full((1, SSD_INNER)), full((LANES, SSD_INNER)),
        ],
        out_specs=[
            pl.BlockSpec((bg, q, SSD_INNER), lambda i, c: (i, c, 0)),
            pl.BlockSpec((bg, CONV_W - 1, CONV_DIM), lambda i, c: (i, 0, 0)),
            pl.BlockSpec((bg, SSD_HEADS, SSD_HEADDIM, SSD_STATE), lambda i, c: (i, 0, 0, 0)),
        ],
        out_shape=[
            jax.ShapeDtypeStruct((nb, length, SSD_INNER), BF16),
            jax.ShapeDtypeStruct((nb, CONV_W - 1, CONV_DIM), F32),
            jax.ShapeDtypeStruct((nb, SSD_HEADS, SSD_HEADDIM, SSD_STATE), F32),
        ],
        scratch_shapes=[
            pltpu.VMEM((bg, q + 8, CONV_DIM), F32),
            pltpu.VMEM((bg, SSD_HEADS, SSD_HEADDIM, SSD_STATE), F32),
            pltpu.VMEM((rows, LANES), F32),
            pltpu.VMEM((rows, LANES), F32),
            pltpu.VMEM((SSD_GROUPS, gw, rows), BF16),
            pltpu.VMEM((SSD_GROUPS, gw, rows), F32),
            pltpu.VMEM((rows, SSD_INNER), F32),
        ],
        compiler_params=_cparams(("parallel", "arbitrary")),
        name="ssd",
    )(p3, p3, p3, p3, conv_prev, h0, conv_w, conv_b, dtb, alog, dsk_x, norm_w, eexp)


def _gla_body(q_ref, k_ref, v_ref, r_ref, dtg_ref, s0_ref, w2_ref, b2_ref, nw_ref, o_ref, sout_ref,
              sst, qe_scr, kdt_scr, v_scr, cdt_scr, oint_scr, *, bg, q):
    rows = bg * q
    c = pl.program_id(1)

    @pl.when(c == 0)
    def _():
        sst[...] = s0_ref[...]

    qv = q_ref[...].reshape(rows, GLA_KEY)
    kv = k_ref[...].reshape(rows, GLA_KEY)
    vv = v_ref[...].reshape(rows, GLA_VAL)
    glr = dtg_ref[...].reshape(rows, LANES)
    g = jax.nn.log_sigmoid(_dot(glr.astype(BF16), w2_ref[...]) + b2_ref[...]) * (1.0 / GLA_GATE_NORM)

    same, causal = _block_masks(rows, q)
    gc = _dot_hi(causal.astype(F32), g)
    gl = _dot_hi(same.astype(F32), g)
    qe = qv * (GLA_DK ** -0.5) * jnp.exp(gc)
    ke = kv * jnp.exp(-gc)
    kd = kv * jnp.exp(gl - gc)
    cdk = jnp.exp(gl)

    qe_b = qe.astype(BF16)
    ke_b = ke.astype(BF16)
    v_b = vv.astype(BF16)
    o_intra = []
    for h in range(GLA_HEADS):
        ks = slice(h * GLA_DK, (h + 1) * GLA_DK)
        att = jnp.where(causal, _dot_nt(qe_b[:, ks], ke_b[:, ks]), 0.0)
        o_intra.append(_dot(att.astype(BF16), v_b[:, h * GLA_DV:(h + 1) * GLA_DV]))
        kdt_scr[h] = kd[:, ks].T.astype(BF16)
        cdt_scr[h] = cdk[:, ks].T
    qe_scr[...] = qe
    v_scr[...] = vv

    def seq_step(b, carry):
        r0 = pl.multiple_of(b * q, q)
        rowm, sel = _row_select(rows, GLA_DV, r0, q)
        for h in range(GLA_HEADS):
            s_old = sst[b, h]
            qeb = qe_scr[pl.ds(r0, q), h * GLA_DK:(h + 1) * GLA_DK]
            oint_scr[pl.ds(r0, q), h * GLA_DV:(h + 1) * GLA_DV] = _dot(qeb.astype(BF16), s_old.astype(BF16))
            vm = jnp.where(rowm, v_scr[:, h * GLA_DV:(h + 1) * GLA_DV], 0.0).astype(BF16)
            s_new = _dot(kdt_scr[h], vm)
            cd = _dot_hi(cdt_scr[h], sel)
            sst[b, h] = cd * s_old + s_new
        return carry

    lax.fori_loop(0, bg, seq_step, 0)

    rv = r_ref[...].reshape(rows, GLA_VAL)
    outs = []
    for h in range(GLA_HEADS):
        vs = slice(h * GLA_DV, (h + 1) * GLA_DV)
        o = o_intra[h] + oint_scr[:, vs]
        ms = jnp.mean(o * o, axis=-1, keepdims=True)
        outs.append(o * lax.rsqrt(ms + EPS) * nw_ref[...])
    o_all = jnp.concatenate(outs, axis=1) * _silu(rv)
    o_ref[...] = o_all.astype(BF16).reshape(bg, q, GLA_VAL)

    @pl.when(c == pl.num_programs(1) - 1)
    def _():
        sout_ref[...] = sst[...]


def _gla_call(p3, s0, w2_pad, b2, norm_w, bg, q):
    nb, length, _ = p3.shape
    full = lambda shape: pl.BlockSpec(shape, lambda i, c: (0,) * len(shape))
    rows = bg * q
    state_block = (bg, GLA_HEADS, GLA_DK, GLA_DV)
    return pl.pallas_call(
        functools.partial(_gla_body, bg=bg, q=q),
        grid=(nb // bg, length // q),
        in_specs=[
            pl.BlockSpec((bg, q, GLA_KEY), lambda i, c: (i, c, COL_Q // GLA_KEY)),
            pl.BlockSpec((bg, q, GLA_KEY), lambda i, c: (i, c, COL_K // GLA_KEY)),
            pl.BlockSpec((bg, q, GLA_VAL), lambda i, c: (i, c, COL_V // GLA_VAL)),
            pl.BlockSpec((bg, q, GLA_VAL), lambda i, c: (i, c, COL_R // GLA_VAL)),
            pl.BlockSpec((bg, q, LANES), lambda i, c: (i, c, COL_DTG // LANES)),
            pl.BlockSpec(state_block, lambda i, c: (i, 0, 0, 0)),
            full((LANES, GLA_KEY)), full((1, GLA_KEY)), full((1, GLA_DV)),
        ],
        out_specs=[
            pl.BlockSpec((bg, q, GLA_VAL), lambda i, c: (i, c, 0)),
            pl.BlockSpec(state_block, lambda i, c: (i, 0, 0, 0)),
        ],
        out_shape=[
            jax.ShapeDtypeStruct((nb, length, GLA_VAL), BF16),
            jax.ShapeDtypeStruct((nb, GLA_HEADS, GLA_DK, GLA_DV), F32),
        ],
        scratch_shapes=[
            pltpu.VMEM(state_block, F32),
            pltpu.VMEM((rows, GLA_KEY), F32),
            pltpu.VMEM((GLA_HEADS, GLA_DK, rows), BF16),
            pltpu.VMEM((rows, GLA_VAL), F32),
            pltpu.VMEM((GLA_HEADS, GLA_DK, rows), F32),
            pltpu.VMEM((rows, GLA_VAL), F32),
        ],
        compiler_params=_cparams(("parallel", "arbitrary")),
        name="gla",
    )(p3, p3, p3, p3, p3, s0, w2_pad, b2, norm_w)


def _merge_body(x_ref, ys_ref, o_ref, gs_ref, gg_ref, wso_ref, wgo_ref, wo_ref, nw_ref, x1_ref, h2t_ref):
    mix = (jax.nn.sigmoid(gs_ref[...]) * _dot(ys_ref[...], wso_ref[...])
           + jax.nn.sigmoid(gg_ref[...]) * _dot(o_ref[...], wgo_ref[...]))
    x1 = x_ref[...] + _dot(mix.astype(BF16), wo_ref[...])
    x1_ref[...] = x1
    ms = jnp.mean(x1 * x1, axis=-1, keepdims=True)
    h2 = x1 * lax.rsqrt(ms + EPS) * nw_ref[...]
    h2t_ref[...] = h2.T.astype(BF16)


def _merge_call(x2d, ys, o, proj, wso, wgo, wo, norm_w, tm):
    n = x2d.shape[0]
    tok = lambda col: pl.BlockSpec((tm, D_MODEL), lambda i: (i, col))
    wfull = pl.BlockSpec((D_MODEL, D_MODEL), lambda i: (0, 0))
    return pl.pallas_call(
        _merge_body,
        grid=(n // tm,),
        in_specs=[tok(0), tok(0), tok(0), tok(COL_GS // D_MODEL), tok(COL_GG // D_MODEL),
                  wfull, wfull, wfull, pl.BlockSpec((1, D_MODEL), lambda i: (0, 0))],
        out_specs=[pl.BlockSpec((tm, D_MODEL), lambda i: (i, 0)),
                   pl.BlockSpec((D_MODEL, tm), lambda i: (0, i))],
        out_shape=[jax.ShapeDtypeStruct((n, D_MODEL), F32),
                   jax.ShapeDtypeStruct((D_MODEL, n), BF16)],
        compiler_params=_cparams(("parallel",)),
        name="merge",
    )(x2d, ys, o, proj, proj, wso, wgo, wo, norm_w)


NEG_INF = float("-inf")
PAIR_ROWS = ((1, 8), (2, 5), (3, 4), (4, 3), (5, 2), (6, 2), (7, 2))
N_PAIR_PAD = 30


def _count_rows(mask):
    return jnp.sum(jnp.where(mask, 1.0, 0.0), axis=0, keepdims=True)


def _top16_fast(s, want_rank):
    row = lax.broadcasted_iota(jnp.int32, (PEER_TOPK, s.shape[1]), 0)
    vals = jnp.zeros((PEER_TOPK, s.shape[1]), F32)
    rank = jnp.full(s.shape, float(PEER_TOPK), F32) if want_rank else None
    cur = s
    for it in range(PEER_TOPK):
        m = jnp.max(cur, axis=0, keepdims=True)
        vals = jnp.where(row == it, m, vals)
        eq = cur == m
        if want_rank:
            rank = jnp.where(eq, float(it), rank)
        cur = jnp.where(eq, NEG_INF, cur)
    return vals, rank, _count_rows(cur == NEG_INF)


def _top16_exact(s):
    row = lax.broadcasted_iota(jnp.int32, (PEER_TOPK, s.shape[1]), 0)
    idx = lax.broadcasted_iota(jnp.int32, s.shape, 0).astype(F32)
    vals = jnp.zeros((PEER_TOPK, s.shape[1]), F32)
    rank = jnp.full(s.shape, float(PEER_TOPK), F32)
    cur = s
    for it in range(PEER_TOPK):
        m = jnp.max(cur, axis=0, keepdims=True)
        vals = jnp.where(row == it, m, vals)
        hit = idx == jnp.min(jnp.where(cur == m, idx, float(s.shape[0])), axis=0, keepdims=True)
        rank = jnp.where(hit, float(it), rank)
        cur = jnp.where(hit, NEG_INF, cur)
    return vals, rank


def _head_fast(s1, s2):
    tt = s1.shape[1]
    row8 = lax.broadcasted_iota(jnp.int32, (SUBLANES, tt), 0)
    v1, _, gone1 = _top16_fast(s1, False)
    v2, rank2, gone2 = _top16_fast(s2, True)
    cands = [v1[0:1] + v2]
    v2lo = v2[0:SUBLANES]
    for a, nb in PAIR_ROWS:
        cands.append(jnp.where(row8 < nb, v1[a:a + 1] + v2lo, NEG_INF))
    cands.append(v1[SUBLANES:PEER_TOPK] + v2[0:1])
    cur = jnp.concatenate(cands, axis=0)
    top = v1[0:1] + v2[0:1]
    zsum = jnp.zeros((1, tt), F32)
    tau = top
    for _ in range(PEER_TOPK):
        tau = jnp.max(cur, axis=0, keepdims=True)
        zsum = zsum + jnp.exp(tau - top)
        cur = jnp.where(cur == tau, NEG_INF, cur)
    gone3 = _count_rows(cur == NEG_INF) - float(N_PAIR_PAD)
    cnt16 = jnp.zeros((PEER_TOPK, tt), F32)
    for b in range(PEER_TOPK):
        cnt16 = cnt16 + jnp.where(v1 + v2[b:b + 1] >= tau, 1.0, 0.0)
    cnt = jnp.zeros((N_KEYS, tt), F32)
    for a in range(PEER_TOPK):
        cnt = jnp.where(s1 == v1[a:a + 1], cnt16[a:a + 1], cnt)
    tables = (rank2, jnp.exp(s2 - v2[0:1]), cnt, jnp.exp(s1 - v1[0:1]) / zsum)
    return tables, jnp.maximum(jnp.maximum(gone1, gone2), gone3)


def _head_exact(s1, s2):
    tt = s1.shape[1]
    v1, rank1 = _top16_exact(s1)
    v2, rank2 = _top16_exact(s2)
    sums = jnp.concatenate([v1[a:a + 1] + v2 for a in range(PEER_TOPK)], axis=0)
    n_pairs = PEER_TOPK * PEER_TOPK
    flat = lax.broadcasted_iota(jnp.int32, (n_pairs, tt), 0).astype(F32)
    top = v1[0:1] + v2[0:1]
    zsum = jnp.zeros((1, tt), F32)
    cur = sums
    tau = top
    last = jnp.zeros((1, tt), F32)
    for _ in range(PEER_TOPK):
        tau = jnp.max(cur, axis=0, keepdims=True)
        last = jnp.min(jnp.where(cur == tau, flat, float(n_pairs)), axis=0, keepdims=True)
        zsum = zsum + jnp.exp(tau - top)
        cur = jnp.where(flat == last, NEG_INF, cur)
    taken = jnp.where(sums > tau, 1.0, jnp.where(sums == tau, jnp.where(flat <= last, 1.0, 0.0), 0.0))
    cnt = jnp.zeros((N_KEYS, tt), F32)
    for a in range(PEER_TOPK):
        cnt_a = jnp.sum(taken[a * PEER_TOPK:(a + 1) * PEER_TOPK], axis=0, keepdims=True)
        cnt = jnp.where(rank1 == float(a), cnt_a, cnt)
    return rank2, jnp.exp(s2 - v2[0:1]), cnt, jnp.exp(s1 - v1[0:1]) / zsum


def _topk_body(h2t_ref, wqt_ref, keys_ref, rk2_ref, e2_ref, cnt_ref, c_ref, s_scr, tie_scr):
    qt = _dot(wqt_ref[...], h2t_ref[...])

    def store(h, tables):
        rank2, e2, cnt, c = tables
        rk2_ref[h] = rank2.astype(BF16)
        e2_ref[h] = e2.astype(BF16)
        cnt_ref[h] = cnt
        c_ref[h] = c

    for h in range(PEER_HEADS):
        base = h * 2 * PEER_HALF
        s1 = _dot(keys_ref[h, 0], qt[base:base + PEER_HALF].astype(BF16))
        s2 = _dot(keys_ref[h, 1], qt[base + PEER_HALF:base + 2 * PEER_HALF].astype(BF16))
        tables, gone = _head_fast(s1, s2)
        store(h, tables)
        s_scr[2 * h] = s1
        s_scr[2 * h + 1] = s2
        tie_scr[h] = (jnp.max(gone) > float(PEER_TOPK)).astype(jnp.int32)

    def redo(h, carry):
        @pl.when(tie_scr[h] > 0)
        def _():
            store(h, _head_exact(s_scr[2 * h], s_scr[2 * h + 1]))
        return carry

    lax.fori_loop(0, PEER_HEADS, redo, 0)


def _topk_call(h2t, wqt, keys, tt):
    n = h2t.shape[1]
    tab = pl.BlockSpec((PEER_HEADS, N_KEYS, tt), lambda i: (0, 0, i))
    tab_f32 = jax.ShapeDtypeStruct((PEER_HEADS, N_KEYS, n), F32)
    tab_b16 = jax.ShapeDtypeStruct((PEER_HEADS, N_KEYS, n), BF16)
    return pl.pallas_call(
        _topk_body,
        grid=(n // tt,),
        in_specs=[pl.BlockSpec((D_MODEL, tt), lambda i: (0, i)),
                  pl.BlockSpec(wqt.shape, lambda i: (0, 0)),
                  pl.BlockSpec(keys.shape, lambda i: (0, 0, 0, 0))],
        out_specs=[tab, tab, tab, tab],
        out_shape=[tab_b16, tab_b16, tab_f32, tab_f32],
        scratch_shapes=[pltpu.VMEM((2 * PEER_HEADS, N_KEYS, tt), F32), pltpu.SMEM((PEER_HEADS,), jnp.int32)],
        compiler_params=_cparams(("parallel",)),
        name="topk",
    )(h2t, wqt, keys)


def _gelu(x):
    x = x.astype(BF16)
    return (0.5 * x) * (1.0 + lax.erf(x * (2.0 ** -0.5)))


def _peer_gate_mul(act_scr, w_scr, rk2_ref, e2_ref, cnt_ref, c_ref, chunk, groups):
    tt = act_scr.shape[1]
    zero = jnp.zeros((), BF16)
    key0 = pl.multiple_of(chunk * groups, groups)
    for ii in range(groups):
        gate = None
        for h in range(PEER_HEADS):
            cnt_row = cnt_ref[h, pl.ds(key0, groups), :][ii:ii + 1]
            c_row = c_ref[h, pl.ds(key0, groups), :][ii:ii + 1]
            cnt_t = jnp.broadcast_to(cnt_row, (N_KEYS, tt)).astype(BF16)
            c_t = jnp.broadcast_to(c_row, (N_KEYS, tt)).astype(BF16)
            term = jnp.where(rk2_ref[h] < cnt_t, e2_ref[h], zero) * c_t
            gate = term if gate is None else gate + term
        rows = slice(ii * N_KEYS, (ii + 1) * N_KEYS)
        w_scr[rows, :] = act_scr[rows, :] * gate


def _peer_body(h2t_ref, x1_ref, rk2_ref, e2_ref, cnt_ref, c_ref, u_ref, vt_ref, nw_ref, y_ref,
               acc_scr, act_scr, w_scr, *, ec):
    e = pl.program_id(1)
    groups = ec // N_KEYS

    @pl.when(e == 0)
    def _():
        acc_scr[...] = jnp.zeros_like(acc_scr)

    act_scr[...] = _gelu(_dot(u_ref[...], h2t_ref[...]))
    _peer_gate_mul(act_scr, w_scr, rk2_ref, e2_ref, cnt_ref, c_ref, e, groups)
    acc_scr[...] += _dot(vt_ref[...], w_scr[...])

    @pl.when(e == pl.num_programs(1) - 1)
    def _():
        x2 = x1_ref[...] + acc_scr[...].T
        ms = jnp.mean(x2 * x2, axis=-1, keepdims=True)
        y_ref[...] = x2 * lax.rsqrt(ms + EPS) * nw_ref[...]


def _peer_call(h2t, x1, rk2, e2, cnt, cc, u_b, vt_b, norm_w, tt, ec):
    n = x1.shape[0]
    assert (ec // N_KEYS) % SUBLANES == 0, "an expert chunk covers whole sublane tiles of first-half keys"
    tab = pl.BlockSpec((PEER_HEADS, N_KEYS, tt), lambda i, e: (0, 0, i))
    chunk_shape = pltpu.VMEM((ec, tt), BF16)
    return pl.pallas_call(
        functools.partial(_peer_body, ec=ec),
        grid=(n // tt, u_b.shape[0] // ec),
        in_specs=[pl.BlockSpec((D_MODEL, tt), lambda i, e: (0, i)),
                  pl.BlockSpec((tt, D_MODEL), lambda i, e: (i, 0)),
                  tab, tab, tab, tab,
                  pl.BlockSpec((ec, D_MODEL), lambda i, e: (e, 0)),
                  pl.BlockSpec((D_MODEL, ec), lambda i, e: (0, e)),
                  pl.BlockSpec((1, D_MODEL), lambda i, e: (0, 0))],
        out_specs=pl.BlockSpec((tt, D_MODEL), lambda i, e: (i, 0)),
        out_shape=jax.ShapeDtypeStruct((n, D_MODEL), F32),
        scratch_shapes=[pltpu.VMEM((D_MODEL, tt), F32), chunk_shape, chunk_shape],
        compiler_params=_cparams(("parallel", "arbitrary")),
        name="peer",
    )(h2t, x1, rk2, e2, cnt, cc, u_b, vt_b, norm_w)


def _permute_w_in(w):
    splits = [sum(IN_SIZES[:i + 1]) for i in range(len(IN_SIZES) - 1)]
    z, xbc, dt, q, k, v, glr, r, gs, gg = jnp.split(w, splits, axis=1)
    pad = jnp.zeros((D_MODEL, PROJ_COLS - COL_DTG - SSD_HEADS - GLA_GATE_RANK), w.dtype)
    return jnp.concatenate([z, v, r, gs, gg, q, k, xbc, dt, glr, pad], axis=1).astype(BF16)


def _row(v, width=None):
    v = v.reshape(1, -1).astype(F32)
    if width is not None and v.shape[1] < width:
        v = jnp.pad(v, ((0, 0), (0, width - v.shape[1])))
    return v


def _trunk(x, conv0, ssd0, gla0, wts, bg_ssd, q_ssd, bg_gla, q_gla, tm, tt, ec):
    nb, length, _ = x.shape
    n = nb * length
    x2d = x.reshape(n, D_MODEL)
    proj = _proj_call(x2d, wts["norm_mix"], wts["w_in"], tm, 1280)
    p3 = proj.reshape(nb, length, PROJ_COLS)
    ys, conv_new, ssd_new = _ssd_call(p3, conv0, ssd0, wts["conv_w"], wts["conv_b"], wts["dtb"], wts["alog"],
                                      wts["dsk"], wts["ssd_norm"], wts["eexp"], bg_ssd, q_ssd)
    o, gla_new = _gla_call(p3, gla0, wts["w2"], wts["b2"], wts["gla_norm"], bg_gla, q_gla)
    x1, h2t = _merge_call(x2d, ys.reshape(n, SSD_INNER), o.reshape(n, GLA_VAL), proj,
                          wts["wso"], wts["wgo"], wts["wo"], wts["norm_ffn"], tm // 2)
    rk2, e2, cnt, cc = _topk_call(h2t, wts["wqt"], wts["keys"], tt)
    y = _peer_call(h2t, x1, rk2, e2, cnt, cc, wts["u"], wts["vt"], wts["final_norm"], tt, ec)
    return y.reshape(nb, length, D_MODEL), conv_new[None], ssd_new[None], gla_new[None]


def _weights(p):
    assert p["norm_mix_w"].shape[0] == 1, "one layer"
    w2_pad = jnp.zeros((LANES, GLA_KEY), F32).at[SSD_HEADS:SSD_HEADS + GLA_GATE_RANK].set(p["w_gla_gate2"][0])
    head_of_lane = jnp.arange(SSD_INNER) // SSD_HEADDIM
    return {
        "norm_mix": _row(p["norm_mix_w"][0]),
        "w_in": _permute_w_in(p["w_in"][0]),
        "conv_w": p["conv_w"][0], "conv_b": _row(p["conv_b"][0]),
        "dtb": _row(p["dt_bias"][0], LANES), "alog": _row(p["a_log"][0], LANES),
        "dsk": _row(jnp.repeat(p["d_skip"][0], SSD_HEADDIM)),
        "ssd_norm": _row(p["ssd_norm_w"][0]),
        "eexp": (jnp.arange(LANES)[:, None] == head_of_lane[None, :]).astype(F32),
        "w2": w2_pad.astype(BF16), "b2": _row(p["b_gla_gate"][0]), "gla_norm": _row(p["gla_norm_w"][0]),
        "wso": p["w_ssd_out"][0].astype(BF16), "wgo": p["w_gla_out"][0].astype(BF16),
        "wo": p["w_out"][0].astype(BF16),
        "norm_ffn": _row(p["norm_ffn_w"][0]),
        "wqt": p["w_query"][0].T.astype(BF16), "keys": p["sub_keys"][0].astype(BF16),
        "u": p["expert_u"][0].astype(BF16), "vt": p["expert_v"][0].T.astype(BF16),
        "final_norm": _row(p["final_norm_w"]),
    }


def kernel(x_prompt, x_sample, state_conv, state_ssd, state_gla, norm_mix_w, w_in, conv_w, conv_b, dt_bias,
           a_log, d_skip, ssd_norm_w, w_gla_gate2, b_gla_gate, gla_norm_w, w_ssd_out, w_gla_out, w_out,
           norm_ffn_w, w_query, sub_keys, expert_u, expert_v, final_norm_w):
    wts = _weights(dict(
        norm_mix_w=norm_mix_w, w_in=w_in, conv_w=conv_w, conv_b=conv_b, dt_bias=dt_bias, a_log=a_log,
        d_skip=d_skip, ssd_norm_w=ssd_norm_w, w_gla_gate2=w_gla_gate2, b_gla_gate=b_gla_gate,
        gla_norm_w=gla_norm_w, w_ssd_out=w_ssd_out, w_gla_out=w_gla_out, w_out=w_out, norm_ffn_w=norm_ffn_w,
        w_query=w_query, sub_keys=sub_keys, expert_u=expert_u, expert_v=expert_v, final_norm_w=final_norm_w))
    bp = x_prompt.shape[0]
    conv0 = jnp.zeros((bp, CONV_W - 1, CONV_DIM), F32)
    ssd0 = jnp.zeros((bp, SSD_HEADS, SSD_HEADDIM, SSD_STATE), F32)
    gla0 = jnp.zeros((bp, GLA_HEADS, GLA_DK, GLA_DV), F32)
    ls = x_sample.shape[1]
    y_p, conv_p, ssd_p, gla_p = _trunk(x_prompt, conv0, ssd0, gla0, wts,
                                       bg_ssd=1, q_ssd=ROWS, bg_gla=2, q_gla=ROWS // 2, tm=1024, tt=512, ec=2048)
    y_s, conv_s, ssd_s, gla_s = _trunk(x_sample, state_conv[0], state_ssd[0], state_gla[0], wts,
                                       bg_ssd=ROWS // ls, q_ssd=ls, bg_gla=ROWS // ls, q_gla=ls,
                                       tm=1024, tt=512, ec=2048)
    return (y_p, y_s, conv_p, ssd_p, gla_p, conv_s, ssd_s, gla_s)
```

```python
import functools
import math

import jax
import jax.numpy as jnp
from jax import lax
from jax.experimental import pallas as pl
from jax.experimental.pallas import tpu as pltpu

F32 = jnp.float32
BF16 = jnp.bfloat16

D_MODEL = 1024
SSD_HEADS = 16
SSD_HEADDIM = 64
SSD_INNER = 1024
SSD_GROUPS = 2
SSD_STATE = 64
CONV_W = 4
CONV_DIM = 1280
GLA_HEADS = 4
GLA_DK = 128
GLA_DV = 256
GLA_KEY = 512
GLA_VAL = 1024
GLA_GATE_RANK = 16
GLA_GATE_NORM = 16.0
IN_SIZES = (SSD_INNER, CONV_DIM, SSD_HEADS, GLA_KEY, GLA_KEY, GLA_VAL, GLA_GATE_RANK, GLA_VAL, D_MODEL, D_MODEL)
PEER_HEADS = 8
N_KEYS = 128
PEER_HALF = 128
PEER_TOPK = 16
EPS = 1e-6

LANES = 128
SUBLANES = 8
ROWS = 128
VMEM_LIMIT = 56 * 1024 * 1024

COL_Z, COL_V, COL_R, COL_GS, COL_GG = 0, 1024, 2048, 3072, 4096
COL_Q, COL_K, COL_XS, COL_BC, COL_DTG = 5120, 5632, 6144, 7168, 7424
PROJ_COLS = 7680

NT_DIMS = (((1,), (1,)), ((), ()))


def _cparams(sem):
    return pltpu.CompilerParams(dimension_semantics=sem, vmem_limit_bytes=VMEM_LIMIT)


def _dot(a, b):
    return jnp.dot(a, b, preferred_element_type=F32)


def _split3(x):
    hi = x.astype(BF16)
    r = x - hi.astype(F32)
    mid = r.astype(BF16)
    return hi, mid, (r - mid.astype(F32)).astype(BF16)


def _dot_sel(sel, x):
    hi, mid, lo = _split3(x)
    sel = sel.astype(BF16)
    return _dot(sel, hi) + (_dot(sel, mid) + _dot(sel, lo))


def _dot_sel_r(x, sel):
    hi, mid, lo = _split3(x)
    sel = sel.astype(BF16)
    return _dot(hi, sel) + (_dot(mid, sel) + _dot(lo, sel))


def _dot_nt(a, b):
    return lax.dot_general(a, b, NT_DIMS, preferred_element_type=F32)


def _silu(x):
    return x * jax.nn.sigmoid(x)


def _proj_body(x_ref, nw_ref, w_ref, o_ref, h_scr):
    @pl.when(pl.program_id(1) == 0)
    def _():
        x = x_ref[...]
        ms = jnp.mean(x * x, axis=-1, keepdims=True)
        h_scr[...] = (x * lax.rsqrt(ms + EPS) * nw_ref[...]).astype(BF16)

    o_ref[...] = _dot(h_scr[...], w_ref[...])


def _proj_call(x2d, norm_w, w_perm, tm, tn):
    n = x2d.shape[0]
    return pl.pallas_call(
        _proj_body,
        grid=(n // tm, PROJ_COLS // tn),
        in_specs=[
            pl.BlockSpec((tm, D_MODEL), lambda i, j: (i, 0)),
            pl.BlockSpec((1, D_MODEL), lambda i, j: (0, 0)),
            pl.BlockSpec((D_MODEL, tn), lambda i, j: (0, j)),
        ],
        out_specs=pl.BlockSpec((tm, tn), lambda i, j: (i, j)),
        out_shape=jax.ShapeDtypeStruct((n, PROJ_COLS), F32),
        scratch_shapes=[pltpu.VMEM((tm, D_MODEL), BF16)],
        compiler_params=_cparams(("parallel", "arbitrary")),
        name="proj",
    )(x2d, norm_w, w_perm)


def _block_masks(rows, q):
    ri = lax.broadcasted_iota(jnp.int32, (rows, rows), 0)
    ci = lax.broadcasted_iota(jnp.int32, (rows, rows), 1)
    shift = int(math.log2(q))
    same = (ri >> shift) == (ci >> shift)
    causal = same & (ci <= ri)
    return same, causal


def _row_select(rows, width, r0, q):
    rcol = lax.broadcasted_iota(jnp.int32, (rows, 1), 0)
    rowm = (rcol >= r0) & (rcol < r0 + q)
    rfull = lax.broadcasted_iota(jnp.int32, (rows, width), 0)
    sel = jnp.where(rfull == r0 + (q - 1), 1.0, 0.0).astype(F32)
    return rowm, sel


def _ssd_body(xs_ref, bc_ref, dtg_ref, z_ref, cprev_ref, h0_ref, cw_ref, cb_ref, dtb_ref, alog_ref,
              dsk_ref, nw_ref, eexp_ref, eexpt_ref, ys_ref, cnew_ref, hout_ref,
              xext, hst, c_scr, b_scr, xwt_scr, cdt_scr, yoff_scr, *, bg, q):
    rows = bg * q
    c = pl.program_id(1)
    ngroup = SSD_HEADS // SSD_GROUPS
    gw = ngroup * SSD_HEADDIM

    @pl.when(c == 0)
    def _():
        xext[:, 5:8, :] = cprev_ref[...]
        hst[...] = h0_ref[...]

    xext[:, 8:8 + q, 0:SSD_INNER] = xs_ref[...]
    xext[:, 8:8 + q, SSD_INNER:CONV_DIM] = bc_ref[...]
    acc = cb_ref[...][None]
    for k in range(CONV_W):
        acc = acc + cw_ref[k:k + 1, :][None] * xext[:, 5 + k:5 + k + q, :]
    xc = _silu(acc).reshape(rows, CONV_DIM)
    tail = xext[:, q + 5:q + 8, :]
    cnew_ref[...] = tail
    xext[:, 5:8, :] = tail

    xs = xc[:, :SSD_INNER]
    bm = xc[:, SSD_INNER:SSD_INNER + LANES]
    cm = xc[:, SSD_INNER + LANES:CONV_DIM]

    lane = lax.broadcasted_iota(jnp.int32, (1, LANES), 1)
    hmask = lane < SSD_HEADS
    dt = jnp.where(hmask, jax.nn.softplus(dtg_ref[...].reshape(rows, LANES) + dtb_ref[...]), 0.0)
    a = dt * jnp.where(hmask, -jnp.exp(alog_ref[...]), 0.0)

    same, causal = _block_masks(rows, q)
    acum = _dot_sel(causal, a)
    alast = _dot_sel(same, a)

    eexp = eexp_ref[...]
    dt_x = _dot(dt.astype(BF16), eexp)
    e_x = _dot(jnp.exp(acum).astype(BF16), eexp)
    te_x = _dot(jnp.exp(alast - acum).astype(BF16), eexp)
    xdt = xs * dt_x
    xw = xdt * te_x

    cbs = []
    for g in range(SSD_GROUPS):
        cb = _dot_nt(cm[:, g * SSD_STATE:(g + 1) * SSD_STATE].astype(BF16),
                     bm[:, g * SSD_STATE:(g + 1) * SSD_STATE].astype(BF16))
        cbs.append(jnp.where(causal, cb, 0.0))
    ydiag = []
    for h in range(SSD_HEADS):
        col = jnp.broadcast_to(acum[:, h:h + 1], (rows, rows))
        dec = jnp.exp(jnp.where(causal, col - col.T, 0.0))
        w = (cbs[h // ngroup] * dec).astype(BF16)
        ydiag.append(_dot(w, xdt[:, h * SSD_HEADDIM:(h + 1) * SSD_HEADDIM].astype(BF16)))
    y = jnp.concatenate(ydiag, axis=1)

    c_scr[...] = cm
    b_scr[...] = bm
    cdt_scr[...] = jnp.exp(alast).T
    for g in range(SSD_GROUPS):
        xwt_scr[g] = xw[:, g * gw:(g + 1) * gw].T.astype(BF16)

    def seq_step(b, carry):
        r0 = pl.multiple_of(b * q, q)
        rowm, sel = _row_select(rows, SSD_STATE, r0, q)
        for g in range(SSD_GROUPS):
            hg = hst[b, g * ngroup:(g + 1) * ngroup].reshape(gw, SSD_STATE)
            cg = c_scr[pl.ds(r0, q), g * SSD_STATE:(g + 1) * SSD_STATE]
            yoff_scr[pl.ds(r0, q), g * gw:(g + 1) * gw] = _dot_nt(cg.astype(BF16), hg.astype(BF16))
            bmask = jnp.where(rowm, b_scr[:, g * SSD_STATE:(g + 1) * SSD_STATE], 0.0).astype(BF16)
            s_new = _dot(xwt_scr[g], bmask)
            cd = _dot_sel(eexpt_ref[g * gw:(g + 1) * gw, :], _dot_sel_r(cdt_scr[...], sel))
            hst[b, g * ngroup:(g + 1) * ngroup] = (cd * hg + s_new).reshape(ngroup, SSD_HEADDIM, SSD_STATE)
        return carry

    lax.fori_loop(0, bg, seq_step, 0)

    y = y + yoff_scr[...] * e_x + dsk_ref[...] * xs
    yz = y * _silu(z_ref[...].reshape(rows, SSD_INNER))
    ms = jnp.mean(yz * yz, axis=-1, keepdims=True)
    ys_ref[...] = (yz * lax.rsqrt(ms + EPS) * nw_ref[...]).astype(BF16).reshape(bg, q, SSD_INNER)

    @pl.when(c == pl.num_programs(1) - 1)
    def _():
        hout_ref[...] = hst[...]


def _ssd_call(p3, conv_prev, h0, conv_w, conv_b, dtb, alog, dsk_x, norm_w, eexp, eexpt, bg, q):
    nb, length, _ = p3.shape
    full = lambda shape: pl.BlockSpec(shape, lambda i, c: (0,) * len(shape))
    rows = bg * q
    gw = SSD_INNER // SSD_GROUPS
    return pl.pallas_call(
        functools.partial(_ssd_body, bg=bg, q=q),
        grid=(nb // bg, length // q),
        in_specs=[
            pl.BlockSpec((bg, q, SSD_INNER), lambda i, c: (i, c, COL_XS // SSD_INNER)),
            pl.BlockSpec((bg, q, 2 * LANES), lambda i, c: (i, c, COL_BC // (2 * LANES))),
            pl.BlockSpec((bg, q, LANES), lambda i, c: (i, c, COL_DTG // LANES)),
            pl.BlockSpec((bg, q, SSD_INNER), lambda i, c: (i, c, COL_Z // SSD_INNER)),
            pl.BlockSpec((bg, CONV_W - 1, CONV_DIM), lambda i, c: (i, 0, 0)),
            pl.BlockSpec((bg, SSD_HEADS, SSD_HEADDIM, SSD_STATE), lambda i, c: (i, 0, 0, 0)),
            full((CONV_W, CONV_DIM)), full((1, CONV_DIM)), full((1, LANES)), full((1, LANES)),
            full((1, SSD_INNER)), full((1, SSD_INNER)), full((LANES, SSD_INNER)), full((SSD_INNER, LANES)),
        ],
        out_specs=[
            pl.BlockSpec((bg, q, SSD_INNER), lambda i, c: (i, c, 0)),
            pl.BlockSpec((bg, CONV_W - 1, CONV_DIM), lambda i, c: (i, 0, 0)),
            pl.BlockSpec((bg, SSD_HEADS, SSD_HEADDIM, SSD_STATE), lambda i, c: (i, 0, 0, 0)),
        ],
        out_shape=[
            jax.ShapeDtypeStruct((nb, length, SSD_INNER), BF16),
            jax.ShapeDtypeStruct((nb, CONV_W - 1, CONV_DIM), F32),
            jax.ShapeDtypeStruct((nb, SSD_HEADS, SSD_HEADDIM, SSD_STATE), F32),
        ],
        scratch_shapes=[
            pltpu.VMEM((bg, q + 8, CONV_DIM), F32),
            pltpu.VMEM((bg, SSD_HEADS, SSD_HEADDIM, SSD_STATE), F32),
            pltpu.VMEM((rows, LANES), F32),
            pltpu.VMEM((rows, LANES), F32),
            pltpu.VMEM((SSD_GROUPS, gw, rows), BF16),
            pltpu.VMEM((LANES, rows), F32),
            pltpu.VMEM((rows, SSD_INNER), F32),
        ],
        compiler_params=_cparams(("parallel", "arbitrary")),
        name="ssd",
    )(p3, p3, p3, p3, conv_prev, h0, conv_w, conv_b, dtb, alog, dsk_x, norm_w, eexp, eexpt)


def _gla_body(q_ref, k_ref, v_ref, r_ref, dtg_ref, s0_ref, w2_ref, b2_ref, nw_ref, o_ref, sout_ref,
              sst, qe_scr, kdt_scr, v_scr, cdt_scr, oint_scr, *, bg, q):
    rows = bg * q
    c = pl.program_id(1)

    @pl.when(c == 0)
    def _():
        sst[...] = s0_ref[...]

    qv = q_ref[...].reshape(rows, GLA_KEY)
    kv = k_ref[...].reshape(rows, GLA_KEY)
    vv = v_ref[...].reshape(rows, GLA_VAL)
    glr = dtg_ref[...].reshape(rows, LANES)
    g = jax.nn.log_sigmoid(_dot(glr.astype(BF16), w2_ref[...]) + b2_ref[...]) * (1.0 / GLA_GATE_NORM)

    same, causal = _block_masks(rows, q)
    gc = _dot_sel(causal, g)
    gl = _dot_sel(same, g)
    qe = qv * (GLA_DK ** -0.5) * jnp.exp(gc)
    ke = kv * jnp.exp(-gc)
    kd = kv * jnp.exp(gl - gc)
    cdk = jnp.exp(gl)

    qe_b = qe.astype(BF16)
    ke_b = ke.astype(BF16)
    v_b = vv.astype(BF16)
    o_intra = []
    for h in range(GLA_HEADS):
        ks = slice(h * GLA_DK, (h + 1) * GLA_DK)
        att = jnp.where(causal, _dot_nt(qe_b[:, ks], ke_b[:, ks]), 0.0)
        o_intra.append(_dot(att.astype(BF16), v_b[:, h * GLA_DV:(h + 1) * GLA_DV]))
        kdt_scr[h] = kd[:, ks].T.astype(BF16)
        cdt_scr[h] = cdk[:, ks].T
    qe_scr[...] = qe
    v_scr[...] = vv

    def seq_step(b, carry):
        r0 = pl.multiple_of(b * q, q)
        rowm, sel = _row_select(rows, LANES, r0, q)
        for h in range(GLA_HEADS):
            s_old = sst[b, h]
            qeb = qe_scr[pl.ds(r0, q), h * GLA_DK:(h + 1) * GLA_DK]
            oint_scr[pl.ds(r0, q), h * GLA_DV:(h + 1) * GLA_DV] = _dot(qeb.astype(BF16), s_old.astype(BF16))
            vm = jnp.where(rowm, v_scr[:, h * GLA_DV:(h + 1) * GLA_DV], 0.0).astype(BF16)
            s_new = _dot(kdt_scr[h], vm)
            cd = _dot_sel_r(cdt_scr[h], sel)
            sst[b, h] = jnp.concatenate([cd] * (GLA_DV // LANES), axis=1) * s_old + s_new
        return carry

    lax.fori_loop(0, bg, seq_step, 0)

    rv = r_ref[...].reshape(rows, GLA_VAL)
    outs = []
    for h in range(GLA_HEADS):
        vs = slice(h * GLA_DV, (h + 1) * GLA_DV)
        o = o_intra[h] + oint_scr[:, vs]
        ms = jnp.mean(o * o, axis=-1, keepdims=True)
        outs.append(o * lax.rsqrt(ms + EPS) * nw_ref[...])
    o_all = jnp.concatenate(outs, axis=1) * _silu(rv)
    o_ref[...] = o_all.astype(BF16).reshape(bg, q, GLA_VAL)

    @pl.when(c == pl.num_programs(1) - 1)
    def _():
        sout_ref[...] = sst[...]


def _gla_call(p3, s0, w2_pad, b2, norm_w, bg, q):
    nb, length, _ = p3.shape
    full = lambda shape: pl.BlockSpec(shape, lambda i, c: (0,) * len(shape))
    rows = bg * q
    state_block = (bg, GLA_HEADS, GLA_DK, GLA_DV)
    return pl.pallas_call(
        functools.partial(_gla_body, bg=bg, q=q),
        grid=(nb // bg, length // q),
        in_specs=[
            pl.BlockSpec((bg, q, GLA_KEY), lambda i, c: (i, c, COL_Q // GLA_KEY)),
            pl.BlockSpec((bg, q, GLA_KEY), lambda i, c: (i, c, COL_K // GLA_KEY)),
            pl.BlockSpec((bg, q, GLA_VAL), lambda i, c: (i, c, COL_V // GLA_VAL)),
            pl.BlockSpec((bg, q, GLA_VAL), lambda i, c: (i, c, COL_R // GLA_VAL)),
            pl.BlockSpec((bg, q, LANES), lambda i, c: (i, c, COL_DTG // LANES)),
            pl.BlockSpec(state_block, lambda i, c: (i, 0, 0, 0)),
            full((LANES, GLA_KEY)), full((1, GLA_KEY)), full((1, GLA_DV)),
        ],
        out_specs=[
            pl.BlockSpec((bg, q, GLA_VAL), lambda i, c: (i, c, 0)),
            pl.BlockSpec(state_block, lambda i, c: (i, 0, 0, 0)),
        ],
        out_shape=[
            jax.ShapeDtypeStruct((nb, length, GLA_VAL), BF16),
            jax.ShapeDtypeStruct((nb, GLA_HEADS, GLA_DK, GLA_DV), F32),
        ],
        scratch_shapes=[
            pltpu.VMEM(state_block, F32),
            pltpu.VMEM((rows, GLA_KEY), F32),
            pltpu.VMEM((GLA_HEADS, GLA_DK, rows), BF16),
            pltpu.VMEM((rows, GLA_VAL), F32),
            pltpu.VMEM((GLA_HEADS, GLA_DK, rows), F32),
            pltpu.VMEM((rows, GLA_VAL), F32),
        ],
        compiler_params=_cparams(("parallel", "arbitrary")),
        name="gla",
    )(p3, p3, p3, p3, p3, s0, w2_pad, b2, norm_w)


def _merge_body(x_ref, ys_ref, o_ref, gs_ref, gg_ref, wso_ref, wgo_ref, wo_ref, nw_ref, x1_ref, h2t_ref):
    mix = (jax.nn.sigmoid(gs_ref[...]) * _dot(ys_ref[...], wso_ref[...])
           + jax.nn.sigmoid(gg_ref[...]) * _dot(o_ref[...], wgo_ref[...]))
    x1 = x_ref[...] + _dot(mix.astype(BF16), wo_ref[...])
    x1_ref[...] = x1
    ms = jnp.mean(x1 * x1, axis=-1, keepdims=True)
    h2 = x1 * lax.rsqrt(ms + EPS) * nw_ref[...]
    h2t_ref[...] = h2.T.astype(BF16)


def _merge_call(x2d, ys, o, proj, wso, wgo, wo, norm_w, tm):
    n = x2d.shape[0]
    tok = lambda col: pl.BlockSpec((tm, D_MODEL), lambda i: (i, col))
    wfull = pl.BlockSpec((D_MODEL, D_MODEL), lambda i: (0, 0))
    return pl.pallas_call(
        _merge_body,
        grid=(n // tm,),
        in_specs=[tok(0), tok(0), tok(0), tok(COL_GS // D_MODEL), tok(COL_GG // D_MODEL),
                  wfull, wfull, wfull, pl.BlockSpec((1, D_MODEL), lambda i: (0, 0))],
        out_specs=[pl.BlockSpec((tm, D_MODEL), lambda i: (i, 0)),
                   pl.BlockSpec((D_MODEL, tm), lambda i: (0, i))],
        out_shape=[jax.ShapeDtypeStruct((n, D_MODEL), F32),
                   jax.ShapeDtypeStruct((D_MODEL, n), BF16)],
        compiler_params=_cparams(("parallel",)),
        name="merge",
    )(x2d, ys, o, proj, proj, wso, wgo, wo, norm_w)


NEG_INF = float("-inf")
PAIR_ROWS = ((1, 8), (2, 5), (3, 4), (4, 3), (5, 2), (6, 2), (7, 2))
N_PAIR_PAD = 30


def _count_rows(mask):
    return jnp.sum(jnp.where(mask, 1.0, 0.0), axis=0, keepdims=True)


def _top16_fast(s, want_rank):
    row = lax.broadcasted_iota(jnp.int32, (PEER_TOPK, s.shape[1]), 0)
    vals = jnp.zeros((PEER_TOPK, s.shape[1]), F32)
    rank = jnp.full(s.shape, float(PEER_TOPK), F32) if want_rank else None
    cur = s
    for it in range(PEER_TOPK):
        m = jnp.max(cur, axis=0, keepdims=True)
        vals = jnp.where(row == it, m, vals)
        eq = cur == m
        if want_rank:
            rank = jnp.where(eq, float(it), rank)
        cur = jnp.where(eq, NEG_INF, cur)
    return vals, rank, _count_rows(cur == NEG_INF)


def _top16_exact(s):
    row = lax.broadcasted_iota(jnp.int32, (PEER_TOPK, s.shape[1]), 0)
    idx = lax.broadcasted_iota(jnp.int32, s.shape, 0).astype(F32)
    vals = jnp.zeros((PEER_TOPK, s.shape[1]), F32)
    rank = jnp.full(s.shape, float(PEER_TOPK), F32)
    cur = s
    for it in range(PEER_TOPK):
        m = jnp.max(cur, axis=0, keepdims=True)
        vals = jnp.where(row == it, m, vals)
        hit = idx == jnp.min(jnp.where(cur == m, idx, float(s.shape[0])), axis=0, keepdims=True)
        rank = jnp.where(hit, float(it), rank)
        cur = jnp.where(hit, NEG_INF, cur)
    return vals, rank


def _head_fast(s1, s2):
    tt = s1.shape[1]
    row8 = lax.broadcasted_iota(jnp.int32, (SUBLANES, tt), 0)
    v1, _, gone1 = _top16_fast(s1, False)
    v2, rank2, gone2 = _top16_fast(s2, True)
    cands = [v1[0:1] + v2]
    v2lo = v2[0:SUBLANES]
    for a, nb in PAIR_ROWS:
        cands.append(jnp.where(row8 < nb, v1[a:a + 1] + v2lo, NEG_INF))
    cands.append(v1[SUBLANES:PEER_TOPK] + v2[0:1])
    cur = jnp.concatenate(cands, axis=0)
    top = v1[0:1] + v2[0:1]
    zsum = jnp.zeros((1, tt), F32)
    tau = top
    for _ in range(PEER_TOPK):
        tau = jnp.max(cur, axis=0, keepdims=True)
        zsum = zsum + jnp.exp(tau - top)
        cur = jnp.where(cur == tau, NEG_INF, cur)
    gone3 = _count_rows(cur == NEG_INF) - float(N_PAIR_PAD)
    cnt16 = jnp.zeros((PEER_TOPK, tt), F32)
    for b in range(PEER_TOPK):
        cnt16 = cnt16 + jnp.where(v1 + v2[b:b + 1] >= tau, 1.0, 0.0)
    cnt = jnp.zeros((N_KEYS, tt), F32)
    for a in range(PEER_TOPK):
        cnt = jnp.where(s1 == v1[a:a + 1], cnt16[a:a + 1], cnt)
    tables = (rank2, jnp.exp(s2 - v2[0:1]), cnt, jnp.exp(s1 - v1[0:1]) / zsum)
    return tables, jnp.maximum(jnp.maximum(gone1, gone2), gone3)


def _head_exact(s1, s2):
    tt = s1.shape[1]
    v1, rank1 = _top16_exact(s1)
    v2, rank2 = _top16_exact(s2)
    sums = jnp.concatenate([v1[a:a + 1] + v2 for a in range(PEER_TOPK)], axis=0)
    n_pairs = PEER_TOPK * PEER_TOPK
    flat = lax.broadcasted_iota(jnp.int32, (n_pairs, tt), 0).astype(F32)
    top = v1[0:1] + v2[0:1]
    zsum = jnp.zeros((1, tt), F32)
    cur = sums
    tau = top
    last = jnp.zeros((1, tt), F32)
    for _ in range(PEER_TOPK):
        tau = jnp.max(cur, axis=0, keepdims=True)
        last = jnp.min(jnp.where(cur == tau, flat, float(n_pairs)), axis=0, keepdims=True)
        zsum = zsum + jnp.exp(tau - top)
        cur = jnp.where(flat == last, NEG_INF, cur)
    taken = jnp.where(sums > tau, 1.0, jnp.where(sums == tau, jnp.where(flat <= last, 1.0, 0.0), 0.0))
    cnt = jnp.zeros((N_KEYS, tt), F32)
    for a in range(PEER_TOPK):
        cnt_a = jnp.sum(taken[a * PEER_TOPK:(a + 1) * PEER_TOPK], axis=0, keepdims=True)
        cnt = jnp.where(rank1 == float(a), cnt_a, cnt)
    return rank2, jnp.exp(s2 - v2[0:1]), cnt, jnp.exp(s1 - v1[0:1]) / zsum


def _topk_body(h2t_ref, wqt_ref, keys_ref, rk2_ref, e2_ref, cnt_ref, c_ref, s_scr, tie_scr):
    qt = _dot(wqt_ref[...], h2t_ref[...])

    def store(h, tables):
        rank2, e2, cnt, c = tables
        rk2_ref[h] = rank2.astype(BF16)
        e2_ref[h] = e2.astype(BF16)
        cnt_ref[h] = cnt
        c_ref[h] = c

    for h in range(PEER_HEADS):
        base = h * 2 * PEER_HALF
        s1 = _dot(keys_ref[h, 0], qt[base:base + PEER_HALF].astype(BF16))
        s2 = _dot(keys_ref[h, 1], qt[base + PEER_HALF:base + 2 * PEER_HALF].astype(BF16))
        tables, gone = _head_fast(s1, s2)
        store(h, tables)
        s_scr[2 * h] = s1
        s_scr[2 * h + 1] = s2
        tie_scr[h] = (jnp.max(gone) > float(PEER_TOPK)).astype(jnp.int32)

    def redo(h, carry):
        @pl.when(tie_scr[h] > 0)
        def _():
            store(h, _head_exact(s_scr[2 * h], s_scr[2 * h + 1]))
        return carry

    lax.fori_loop(0, PEER_HEADS, redo, 0)


def _topk_call(h2t, wqt, keys, tt):
    n = h2t.shape[1]
    tab = pl.BlockSpec((PEER_HEADS, N_KEYS, tt), lambda i: (0, 0, i))
    tab_f32 = jax.ShapeDtypeStruct((PEER_HEADS, N_KEYS, n), F32)
    tab_b16 = jax.ShapeDtypeStruct((PEER_HEADS, N_KEYS, n), BF16)
    return pl.pallas_call(
        _topk_body,
        grid=(n // tt,),
        in_specs=[pl.BlockSpec((D_MODEL, tt), lambda i: (0, i)),
                  pl.BlockSpec(wqt.shape, lambda i: (0, 0)),
                  pl.BlockSpec(keys.shape, lambda i: (0, 0, 0, 0))],
        out_specs=[tab, tab, tab, tab],
        out_shape=[tab_b16, tab_b16, tab_f32, tab_f32],
        scratch_shapes=[pltpu.VMEM((2 * PEER_HEADS, N_KEYS, tt), F32), pltpu.SMEM((PEER_HEADS,), jnp.int32)],
        compiler_params=_cparams(("parallel",)),
        name="topk",
    )(h2t, wqt, keys)


def _gelu(x):
    x = x.astype(BF16)
    return (0.5 * x) * (1.0 + lax.erf(x * (2.0 ** -0.5)))


def _peer_gate_mul(act_scr, w_scr, rk2_ref, e2_ref, cnt_ref, c_ref, chunk, groups):
    tt = act_scr.shape[1]
    zero = jnp.zeros((), BF16)
    key0 = pl.multiple_of(chunk * groups, groups)
    for ii in range(groups):
        gate = None
        for h in range(PEER_HEADS):
            cnt_row = cnt_ref[h, pl.ds(key0, groups), :][ii:ii + 1]
            c_row = c_ref[h, pl.ds(key0, groups), :][ii:ii + 1]
            cnt_t = jnp.broadcast_to(cnt_row, (N_KEYS, tt)).astype(BF16)
            c_t = jnp.broadcast_to(c_row, (N_KEYS, tt)).astype(BF16)
            term = jnp.where(rk2_ref[h] < cnt_t, e2_ref[h], zero) * c_t
            gate = term if gate is None else gate + term
        rows = slice(ii * N_KEYS, (ii + 1) * N_KEYS)
        w_scr[rows, :] = act_scr[rows, :] * gate


def _peer_body(h2t_ref, x1_ref, rk2_ref, e2_ref, cnt_ref, c_ref, u_ref, vt_ref, nw_ref, y_ref,
               acc_scr, act_scr, w_scr, *, ec):
    e = pl.program_id(1)
    groups = ec // N_KEYS

    @pl.when(e == 0)
    def _():
        acc_scr[...] = jnp.zeros_like(acc_scr)

    act_scr[...] = _gelu(_dot(u_ref[...], h2t_ref[...]))
    _peer_gate_mul(act_scr, w_scr, rk2_ref, e2_ref, cnt_ref, c_ref, e, groups)
    acc_scr[...] += _dot(vt_ref[...], w_scr[...])

    @pl.when(e == pl.num_programs(1) - 1)
    def _():
        x2 = x1_ref[...] + acc_scr[...].T
        ms = jnp.mean(x2 * x2, axis=-1, keepdims=True)
        y_ref[...] = x2 * lax.rsqrt(ms + EPS) * nw_ref[...]


def _peer_call(h2t, x1, rk2, e2, cnt, cc, u_b, vt_b, norm_w, tt, ec):
    n = x1.shape[0]
    assert (ec // N_KEYS) % SUBLANES == 0, "an expert chunk covers whole sublane tiles of first-half keys"
    tab = pl.BlockSpec((PEER_HEADS, N_KEYS, tt), lambda i, e: (0, 0, i))
    chunk_shape = pltpu.VMEM((ec, tt), BF16)
    return pl.pallas_call(
        functools.partial(_peer_body, ec=ec),
        grid=(n // tt, u_b.shape[0] // ec),
        in_specs=[pl.BlockSpec((D_MODEL, tt), lambda i, e: (0, i)),
                  pl.BlockSpec((tt, D_MODEL), lambda i, e: (i, 0)),
                  tab, tab, tab, tab,
                  pl.BlockSpec((ec, D_MODEL), lambda i, e: (e, 0)),
                  pl.BlockSpec((D_MODEL, ec), lambda i, e: (0, e)),
                  pl.BlockSpec((1, D_MODEL), lambda i, e: (0, 0))],
        out_specs=pl.BlockSpec((tt, D_MODEL), lambda i, e: (i, 0)),
        out_shape=jax.ShapeDtypeStruct((n, D_MODEL), F32),
        scratch_shapes=[pltpu.VMEM((D_MODEL, tt), F32), chunk_shape, chunk_shape],
        compiler_params=_cparams(("parallel", "arbitrary")),
        name="peer",
    )(h2t, x1, rk2, e2, cnt, cc, u_b, vt_b, norm_w)


def _permute_w_in(w):
    splits = [sum(IN_SIZES[:i + 1]) for i in range(len(IN_SIZES) - 1)]
    z, xbc, dt, q, k, v, glr, r, gs, gg = jnp.split(w, splits, axis=1)
    pad = jnp.zeros((D_MODEL, PROJ_COLS - COL_DTG - SSD_HEADS - GLA_GATE_RANK), w.dtype)
    return jnp.concatenate([z, v, r, gs, gg, q, k, xbc, dt, glr, pad], axis=1).astype(BF16)


def _row(v, width=None):
    v = v.reshape(1, -1).astype(F32)
    if width is not None and v.shape[1] < width:
        v = jnp.pad(v, ((0, 0), (0, width - v.shape[1])))
    return v


def _trunk(x, conv0, ssd0, gla0, wts, bg_ssd, q_ssd, bg_gla, q_gla, tm, tt, ec):
    nb, length, _ = x.shape
    n = nb * length
    x2d = x.reshape(n, D_MODEL)
    proj = _proj_call(x2d, wts["norm_mix"], wts["w_in"], tm, 1280)
    p3 = proj.reshape(nb, length, PROJ_COLS)
    ys, conv_new, ssd_new = _ssd_call(p3, conv0, ssd0, wts["conv_w"], wts["conv_b"], wts["dtb"], wts["alog"],
                                      wts["dsk"], wts["ssd_norm"], wts["eexp"], wts["eexpt"], bg_ssd, q_ssd)
    o, gla_new = _gla_call(p3, gla0, wts["w2"], wts["b2"], wts["gla_norm"], bg_gla, q_gla)
    x1, h2t = _merge_call(x2d, ys.reshape(n, SSD_INNER), o.reshape(n, GLA_VAL), proj,
                          wts["wso"], wts["wgo"], wts["wo"], wts["norm_ffn"], tm // 2)
    rk2, e2, cnt, cc = _topk_call(h2t, wts["wqt"], wts["keys"], tt)
    y = _peer_call(h2t, x1, rk2, e2, cnt, cc, wts["u"], wts["vt"], wts["final_norm"], tt, ec)
    return y.reshape(nb, length, D_MODEL), conv_new[None], ssd_new[None], gla_new[None]


def _weights(p):
    assert p["norm_mix_w"].shape[0] == 1, "one layer"
    w2_pad = jnp.zeros((LANES, GLA_KEY), F32).at[SSD_HEADS:SSD_HEADS + GLA_GATE_RANK].set(p["w_gla_gate2"][0])
    head_of_lane = jnp.arange(SSD_INNER) // SSD_HEADDIM
    return {
        "norm_mix": _row(p["norm_mix_w"][0]),
        "w_in": _permute_w_in(p["w_in"][0]),
        "conv_w": p["conv_w"][0], "conv_b": _row(p["conv_b"][0]),
        "dtb": _row(p["dt_bias"][0], LANES), "alog": _row(p["a_log"][0], LANES),
        "dsk": _row(jnp.repeat(p["d_skip"][0], SSD_HEADDIM)),
        "ssd_norm": _row(p["ssd_norm_w"][0]),
        "eexp": (jnp.arange(LANES)[:, None] == head_of_lane[None, :]).astype(BF16),
        "eexpt": (head_of_lane[:, None] == jnp.arange(LANES)[None, :]).astype(BF16),
        "w2": w2_pad.astype(BF16), "b2": _row(p["b_gla_gate"][0]), "gla_norm": _row(p["gla_norm_w"][0]),
        "wso": p["w_ssd_out"][0].astype(BF16), "wgo": p["w_gla_out"][0].astype(BF16),
        "wo": p["w_out"][0].astype(BF16),
        "norm_ffn": _row(p["norm_ffn_w"][0]),
        "wqt": p["w_query"][0].T.astype(BF16), "keys": p["sub_keys"][0].astype(BF16),
        "u": p["expert_u"][0].astype(BF16), "vt": p["expert_v"][0].T.astype(BF16),
        "final_norm": _row(p["final_norm_w"]),
    }


def kernel(x_prompt, x_sample, state_conv, state_ssd, state_gla, norm_mix_w, w_in, conv_w, conv_b, dt_bias,
           a_log, d_skip, ssd_norm_w, w_gla_gate2, b_gla_gate, gla_norm_w, w_ssd_out, w_gla_out, w_out,
           norm_ffn_w, w_query, sub_keys, expert_u, expert_v, final_norm_w):
    wts = _weights(dict(
        norm_mix_w=norm_mix_w, w_in=w_in, conv_w=conv_w, conv_b=conv_b, dt_bias=dt_bias, a_log=a_log,
        d_skip=d_skip, ssd_norm_w=ssd_norm_w, w_gla_gate2=w_gla_gate2, b_gla_gate=b_gla_gate,
        gla_norm_w=gla_norm_w, w_ssd_out=w_ssd_out, w_gla_out=w_gla_out, w_out=w_out, norm_ffn_w=norm_ffn_w,
        w_query=w_query, sub_keys=sub_keys, expert_u=expert_u, expert_v=expert_v, final_norm_w=final_norm_w))
    bp = x_prompt.shape[0]
    conv0 = jnp.zeros((bp, CONV_W - 1, CONV_DIM), F32)
    ssd0 = jnp.zeros((bp, SSD_HEADS, SSD_HEADDIM, SSD_STATE), F32)
    gla0 = jnp.zeros((bp, GLA_HEADS, GLA_DK, GLA_DV), F32)
    ls = x_sample.shape[1]
    y_p, conv_p, ssd_p, gla_p = _trunk(x_prompt, conv0, ssd0, gla0, wts,
                                       bg_ssd=1, q_ssd=ROWS, bg_gla=2, q_gla=ROWS // 2, tm=1024, tt=512, ec=2048)
    y_s, conv_s, ssd_s, gla_s = _trunk(x_sample, state_conv[0], state_ssd[0], state_gla[0], wts,
                                       bg_ssd=ROWS // ls, q_ssd=ls, bg_gla=ROWS // ls, q_gla=ls,
                                       tm=1024, tt=512, ec=2048)
    return (y_p, y_s, conv_p, ssd_p, gla_p, conv_s, ssd_s, gla_s)
```

```python
import functools
import math

import jax
import jax.numpy as jnp
from jax import lax
from jax.experimental import pallas as pl
from jax.experimental.pallas import tpu as pltpu

F32 = jnp.float32
BF16 = jnp.bfloat16

D_MODEL = 1024
SSD_HEADS = 16
SSD_HEADDIM = 64
SSD_INNER = 1024
SSD_GROUPS = 2
SSD_STATE = 64
CONV_W = 4
CONV_DIM = 1280
GLA_HEADS = 4
GLA_DK = 128
GLA_DV = 256
GLA_KEY = 512
GLA_VAL = 1024
GLA_GATE_RANK = 16
GLA_GATE_NORM = 16.0
IN_SIZES = (SSD_INNER, CONV_DIM, SSD_HEADS, GLA_KEY, GLA_KEY, GLA_VAL, GLA_GATE_RANK, GLA_VAL, D_MODEL, D_MODEL)
PEER_HEADS = 8
N_KEYS = 128
PEER_HALF = 128
PEER_TOPK = 16
EPS = 1e-6

LANES = 128
SUBLANES = 8
ROWS = 128
VMEM_LIMIT = 56 * 1024 * 1024

COL_Z, COL_V, COL_R, COL_GS, COL_GG = 0, 1024, 2048, 3072, 4096
COL_Q, COL_K, COL_XS, COL_BC, COL_DTG = 5120, 5632, 6144, 7168, 7424
PROJ_COLS = 7680

NT_DIMS = (((1,), (1,)), ((), ()))


def _cparams(sem):
    return pltpu.CompilerParams(dimension_semantics=sem, vmem_limit_bytes=VMEM_LIMIT)


def _dot(a, b):
    return jnp.dot(a, b, preferred_element_type=F32)


def _split3(x):
    hi = x.astype(BF16)
    r = x - hi.astype(F32)
    mid = r.astype(BF16)
    return hi, mid, (r - mid.astype(F32)).astype(BF16)


def _dot_sel(sel, x):
    hi, mid, lo = _split3(x)
    sel = sel.astype(BF16)
    return _dot(sel, hi) + (_dot(sel, mid) + _dot(sel, lo))


def _dot_sel_r(x, sel):
    hi, mid, lo = _split3(x)
    sel = sel.astype(BF16)
    return _dot(hi, sel) + (_dot(mid, sel) + _dot(lo, sel))


def _dot_nt(a, b):
    return lax.dot_general(a, b, NT_DIMS, preferred_element_type=F32)


def _silu(x):
    return x * jax.nn.sigmoid(x)


def _proj_body(x_ref, nw_ref, w_ref, o_ref, dtg_ref, h_scr, *, dtg_tile, dtg_off):
    @pl.when(pl.program_id(1) == 0)
    def _():
        x = x_ref[...]
        ms = jnp.mean(x * x, axis=-1, keepdims=True)
        h_scr[...] = (x * lax.rsqrt(ms + EPS) * nw_ref[...]).astype(BF16)

    acc = _dot(h_scr[...], w_ref[...])
    o_ref[...] = acc.astype(o_ref.dtype)

    @pl.when(pl.program_id(1) == dtg_tile)
    def _():
        dtg_ref[...] = acc[:, dtg_off:dtg_off + LANES]


def _proj_call(x2d, norm_w, w_perm, tm, tn, out_dtype):
    n = x2d.shape[0]
    dtg_tile, dtg_off = divmod(COL_DTG, tn)
    assert dtg_off + LANES <= tn
    return pl.pallas_call(
        functools.partial(_proj_body, dtg_tile=dtg_tile, dtg_off=dtg_off),
        grid=(n // tm, PROJ_COLS // tn),
        in_specs=[
            pl.BlockSpec((tm, D_MODEL), lambda i, j: (i, 0)),
            pl.BlockSpec((1, D_MODEL), lambda i, j: (0, 0)),
            pl.BlockSpec((D_MODEL, tn), lambda i, j: (0, j)),
        ],
        out_specs=[pl.BlockSpec((tm, tn), lambda i, j: (i, j)),
                   pl.BlockSpec((tm, LANES), lambda i, j: (i, 0))],
        out_shape=[jax.ShapeDtypeStruct((n, PROJ_COLS), out_dtype),
                   jax.ShapeDtypeStruct((n, LANES), F32)],
        scratch_shapes=[pltpu.VMEM((tm, D_MODEL), BF16)],
        compiler_params=_cparams(("parallel", "arbitrary")),
        name="proj",
    )(x2d, norm_w, w_perm)


def _block_masks(rows, q):
    ri = lax.broadcasted_iota(jnp.int32, (rows, rows), 0)
    ci = lax.broadcasted_iota(jnp.int32, (rows, rows), 1)
    shift = int(math.log2(q))
    same = (ri >> shift) == (ci >> shift)
    causal = same & (ci <= ri)
    return same, causal


def _row_select(rows, width, r0, q):
    rcol = lax.broadcasted_iota(jnp.int32, (rows, 1), 0)
    rowm = (rcol >= r0) & (rcol < r0 + q)
    rfull = lax.broadcasted_iota(jnp.int32, (rows, width), 0)
    sel = jnp.where(rfull == r0 + (q - 1), 1.0, 0.0).astype(F32)
    return rowm, sel


def _ssd_body(xs_ref, bc_ref, dtg_ref, z_ref, cprev_ref, h0_ref, cw_ref, cb_ref, dtb_ref, alog_ref,
              dsk_ref, nw_ref, eexp_ref, eexpt_ref, ys_ref, cnew_ref, hout_ref,
              xext, hst, c_scr, b_scr, xwt_scr, cdt_scr, yoff_scr, *, bg, q):
    rows = bg * q
    c = pl.program_id(1)
    ngroup = SSD_HEADS // SSD_GROUPS
    gw = ngroup * SSD_HEADDIM

    @pl.when(c == 0)
    def _():
        xext[:, 5:8, :] = cprev_ref[...]
        hst[...] = h0_ref[...]

    xext[:, 8:8 + q, 0:SSD_INNER] = xs_ref[...].astype(F32)
    xext[:, 8:8 + q, SSD_INNER:CONV_DIM] = bc_ref[...].astype(F32)
    acc = cb_ref[...][None]
    for k in range(CONV_W):
        acc = acc + cw_ref[k:k + 1, :][None] * xext[:, 5 + k:5 + k + q, :]
    xc = _silu(acc).reshape(rows, CONV_DIM)
    tail = xext[:, q + 5:q + 8, :]
    cnew_ref[...] = tail
    xext[:, 5:8, :] = tail

    xs = xc[:, :SSD_INNER]
    bm = xc[:, SSD_INNER:SSD_INNER + LANES]
    cm = xc[:, SSD_INNER + LANES:CONV_DIM]

    lane = lax.broadcasted_iota(jnp.int32, (1, LANES), 1)
    hmask = lane < SSD_HEADS
    dt = jnp.where(hmask, jax.nn.softplus(dtg_ref[...].reshape(rows, LANES) + dtb_ref[...]), 0.0)
    a = dt * jnp.where(hmask, -jnp.exp(alog_ref[...]), 0.0)

    same, causal = _block_masks(rows, q)
    acum = _dot_sel(causal, a)
    alast = _dot_sel(same, a)

    eexp = eexp_ref[...]
    dt_x = _dot(dt.astype(BF16), eexp)
    e_x = _dot(jnp.exp(acum).astype(BF16), eexp)
    te_x = _dot(jnp.exp(alast - acum).astype(BF16), eexp)
    xdt = xs * dt_x
    xw = xdt * te_x

    cbs = []
    for g in range(SSD_GROUPS):
        cb = _dot_nt(cm[:, g * SSD_STATE:(g + 1) * SSD_STATE].astype(BF16),
                     bm[:, g * SSD_STATE:(g + 1) * SSD_STATE].astype(BF16))
        cbs.append(jnp.where(causal, cb, 0.0))
    ydiag = []
    for h in range(SSD_HEADS):
        col = jnp.broadcast_to(acum[:, h:h + 1], (rows, rows))
        dec = jnp.exp(jnp.where(causal, col - col.T, 0.0))
        w = (cbs[h // ngroup] * dec).astype(BF16)
        ydiag.append(_dot(w, xdt[:, h * SSD_HEADDIM:(h + 1) * SSD_HEADDIM].astype(BF16)))
    y = jnp.concatenate(ydiag, axis=1)

    c_scr[...] = cm
    b_scr[...] = bm
    cdt_scr[...] = jnp.exp(alast).T
    for g in range(SSD_GROUPS):
        xwt_scr[g] = xw[:, g * gw:(g + 1) * gw].T.astype(BF16)

    def seq_step(b, carry):
        r0 = pl.multiple_of(b * q, q)
        rowm, sel = _row_select(rows, SSD_STATE, r0, q)
        for g in range(SSD_GROUPS):
            hg = hst[b, g * ngroup:(g + 1) * ngroup].reshape(gw, SSD_STATE)
            cg = c_scr[pl.ds(r0, q), g * SSD_STATE:(g + 1) * SSD_STATE]
            yoff_scr[pl.ds(r0, q), g * gw:(g + 1) * gw] = _dot_nt(cg.astype(BF16), hg.astype(BF16))
            bmask = jnp.where(rowm, b_scr[:, g * SSD_STATE:(g + 1) * SSD_STATE], 0.0).astype(BF16)
            s_new = _dot(xwt_scr[g], bmask)
            cd = _dot_sel(eexpt_ref[g * gw:(g + 1) * gw, :], _dot_sel_r(cdt_scr[...], sel))
            hst[b, g * ngroup:(g + 1) * ngroup] = (cd * hg + s_new).reshape(ngroup, SSD_HEADDIM, SSD_STATE)
        return carry

    lax.fori_loop(0, bg, seq_step, 0)

    y = y + yoff_scr[...] * e_x + dsk_ref[...] * xs
    yz = y * _silu(z_ref[...].astype(F32).reshape(rows, SSD_INNER))
    ms = jnp.mean(yz * yz, axis=-1, keepdims=True)
    ys_ref[...] = (yz * lax.rsqrt(ms + EPS) * nw_ref[...]).astype(BF16).reshape(bg, q, SSD_INNER)

    @pl.when(c == pl.num_programs(1) - 1)
    def _():
        hout_ref[...] = hst[...]


def _ssd_call(p3, dtg3, conv_prev, h0, conv_w, conv_b, dtb, alog, dsk_x, norm_w, eexp, eexpt, bg, q):
    nb, length, _ = p3.shape
    full = lambda shape: pl.BlockSpec(shape, lambda i, c: (0,) * len(shape))
    rows = bg * q
    gw = SSD_INNER // SSD_GROUPS
    return pl.pallas_call(
        functools.partial(_ssd_body, bg=bg, q=q),
        grid=(nb // bg, length // q),
        in_specs=[
            pl.BlockSpec((bg, q, SSD_INNER), lambda i, c: (i, c, COL_XS // SSD_INNER)),
            pl.BlockSpec((bg, q, 2 * LANES), lambda i, c: (i, c, COL_BC // (2 * LANES))),
            pl.BlockSpec((bg, q, LANES), lambda i, c: (i, c, 0)),
            pl.BlockSpec((bg, q, SSD_INNER), lambda i, c: (i, c, COL_Z // SSD_INNER)),
            pl.BlockSpec((bg, CONV_W - 1, CONV_DIM), lambda i, c: (i, 0, 0)),
            pl.BlockSpec((bg, SSD_HEADS, SSD_HEADDIM, SSD_STATE), lambda i, c: (i, 0, 0, 0)),
            full((CONV_W, CONV_DIM)), full((1, CONV_DIM)), full((1, LANES)), full((1, LANES)),
            full((1, SSD_INNER)), full((1, SSD_INNER)), full((LANES, SSD_INNER)), full((SSD_INNER, LANES)),
        ],
        out_specs=[
            pl.BlockSpec((bg, q, SSD_INNER), lambda i, c: (i, c, 0)),
            pl.BlockSpec((bg, CONV_W - 1, CONV_DIM), lambda i, c: (i, 0, 0)),
            pl.BlockSpec((bg, SSD_HEADS, SSD_HEADDIM, SSD_STATE), lambda i, c: (i, 0, 0, 0)),
        ],
        out_shape=[
            jax.ShapeDtypeStruct((nb, length, SSD_INNER), BF16),
            jax.ShapeDtypeStruct((nb, CONV_W - 1, CONV_DIM), F32),
            jax.ShapeDtypeStruct((nb, SSD_HEADS, SSD_HEADDIM, SSD_STATE), F32),
        ],
        scratch_shapes=[
            pltpu.VMEM((bg, q + 8, CONV_DIM), F32),
            pltpu.VMEM((bg, SSD_HEADS, SSD_HEADDIM, SSD_STATE), F32),
            pltpu.VMEM((rows, LANES), F32),
            pltpu.VMEM((rows, LANES), F32),
            pltpu.VMEM((SSD_GROUPS, gw, rows), BF16),
            pltpu.VMEM((LANES, rows), F32),
            pltpu.VMEM((rows, SSD_INNER), F32),
        ],
        compiler_params=_cparams(("parallel", "arbitrary")),
        name="ssd",
    )(p3, p3, dtg3, p3, conv_prev, h0, conv_w, conv_b, dtb, alog, dsk_x, norm_w, eexp, eexpt)


def _gla_body(q_ref, k_ref, v_ref, r_ref, dtg_ref, s0_ref, w2_ref, b2_ref, nw_ref, o_ref, sout_ref,
              sst, qe_scr, kdt_scr, v_scr, cdt_scr, oint_scr, *, bg, q):
    rows = bg * q
    c = pl.program_id(1)

    @pl.when(c == 0)
    def _():
        sst[...] = s0_ref[...]

    qv = q_ref[...].astype(F32).reshape(rows, GLA_KEY)
    kv = k_ref[...].astype(F32).reshape(rows, GLA_KEY)
    vv = v_ref[...].astype(F32).reshape(rows, GLA_VAL)
    glr = dtg_ref[...].reshape(rows, LANES)
    g = jax.nn.log_sigmoid(_dot(glr.astype(BF16), w2_ref[...]) + b2_ref[...]) * (1.0 / GLA_GATE_NORM)

    same, causal = _block_masks(rows, q)
    gc = _dot_sel(causal, g)
    gl = _dot_sel(same, g)
    qe = qv * (GLA_DK ** -0.5) * jnp.exp(gc)
    ke = kv * jnp.exp(-gc)
    kd = kv * jnp.exp(gl - gc)
    cdk = jnp.exp(gl)

    qe_b = qe.astype(BF16)
    ke_b = ke.astype(BF16)
    v_b = vv.astype(BF16)
    o_intra = []
    for h in range(GLA_HEADS):
        ks = slice(h * GLA_DK, (h + 1) * GLA_DK)
        att = jnp.where(causal, _dot_nt(qe_b[:, ks], ke_b[:, ks]), 0.0)
        o_intra.append(_dot(att.astype(BF16), v_b[:, h * GLA_DV:(h + 1) * GLA_DV]))
        kdt_scr[h] = kd[:, ks].T.astype(BF16)
        cdt_scr[h] = cdk[:, ks].T
    qe_scr[...] = qe
    v_scr[...] = vv

    def seq_step(b, carry):
        r0 = pl.multiple_of(b * q, q)
        rowm, sel = _row_select(rows, LANES, r0, q)
        for h in range(GLA_HEADS):
            s_old = sst[b, h]
            qeb = qe_scr[pl.ds(r0, q), h * GLA_DK:(h + 1) * GLA_DK]
            oint_scr[pl.ds(r0, q), h * GLA_DV:(h + 1) * GLA_DV] = _dot(qeb.astype(BF16), s_old.astype(BF16))
            vm = jnp.where(rowm, v_scr[:, h * GLA_DV:(h + 1) * GLA_DV], 0.0).astype(BF16)
            s_new = _dot(kdt_scr[h], vm)
            cd = _dot_sel_r(cdt_scr[h], sel)
            sst[b, h] = jnp.concatenate([cd] * (GLA_DV // LANES), axis=1) * s_old + s_new
        return carry

    lax.fori_loop(0, bg, seq_step, 0)

    rv = r_ref[...].astype(F32).reshape(rows, GLA_VAL)
    outs = []
    for h in range(GLA_HEADS):
        vs = slice(h * GLA_DV, (h + 1) * GLA_DV)
        o = o_intra[h] + oint_scr[:, vs]
        ms = jnp.mean(o * o, axis=-1, keepdims=True)
        outs.append(o * lax.rsqrt(ms + EPS) * nw_ref[...])
    o_all = jnp.concatenate(outs, axis=1) * _silu(rv)
    o_ref[...] = o_all.astype(BF16).reshape(bg, q, GLA_VAL)

    @pl.when(c == pl.num_programs(1) - 1)
    def _():
        sout_ref[...] = sst[...]


def _gla_call(p3, dtg3, s0, w2_pad, b2, norm_w, bg, q):
    nb, length, _ = p3.shape
    full = lambda shape: pl.BlockSpec(shape, lambda i, c: (0,) * len(shape))
    rows = bg * q
    state_block = (bg, GLA_HEADS, GLA_DK, GLA_DV)
    return pl.pallas_call(
        functools.partial(_gla_body, bg=bg, q=q),
        grid=(nb // bg, length // q),
        in_specs=[
            pl.BlockSpec((bg, q, GLA_KEY), lambda i, c: (i, c, COL_Q // GLA_KEY)),
            pl.BlockSpec((bg, q, GLA_KEY), lambda i, c: (i, c, COL_K // GLA_KEY)),
            pl.BlockSpec((bg, q, GLA_VAL), lambda i, c: (i, c, COL_V // GLA_VAL)),
            pl.BlockSpec((bg, q, GLA_VAL), lambda i, c: (i, c, COL_R // GLA_VAL)),
            pl.BlockSpec((bg, q, LANES), lambda i, c: (i, c, 0)),
            pl.BlockSpec(state_block, lambda i, c: (i, 0, 0, 0)),
            full((LANES, GLA_KEY)), full((1, GLA_KEY)), full((1, GLA_DV)),
        ],
        out_specs=[
            pl.BlockSpec((bg, q, GLA_VAL), lambda i, c: (i, c, 0)),
            pl.BlockSpec(state_block, lambda i, c: (i, 0, 0, 0)),
        ],
        out_shape=[
            jax.ShapeDtypeStruct((nb, length, GLA_VAL), BF16),
            jax.ShapeDtypeStruct((nb, GLA_HEADS, GLA_DK, GLA_DV), F32),
        ],
        scratch_shapes=[
            pltpu.VMEM(state_block, F32),
            pltpu.VMEM((rows, GLA_KEY), F32),
            pltpu.VMEM((GLA_HEADS, GLA_DK, rows), BF16),
            pltpu.VMEM((rows, GLA_VAL), F32),
            pltpu.VMEM((GLA_HEADS, GLA_DK, rows), F32),
            pltpu.VMEM((rows, GLA_VAL), F32),
        ],
        compiler_params=_cparams(("parallel", "arbitrary")),
        name="gla",
    )(p3, p3, p3, p3, dtg3, s0, w2_pad, b2, norm_w)


def _merge_body(x_ref, ys_ref, o_ref, gs_ref, gg_ref, wso_ref, wgo_ref, wo_ref, nw_ref, x1_ref, h2t_ref):
    mix = (jax.nn.sigmoid(gs_ref[...].astype(F32)) * _dot(ys_ref[...], wso_ref[...])
           + jax.nn.sigmoid(gg_ref[...].astype(F32)) * _dot(o_ref[...], wgo_ref[...]))
    x1 = x_ref[...] + _dot(mix.astype(BF16), wo_ref[...])
    x1_ref[...] = x1
    ms = jnp.mean(x1 * x1, axis=-1, keepdims=True)
    h2 = x1 * lax.rsqrt(ms + EPS) * nw_ref[...]
    h2t_ref[...] = h2.T.astype(BF16)


def _merge_call(x2d, ys, o, proj, wso, wgo, wo, norm_w, tm):
    n = x2d.shape[0]
    tok = lambda col: pl.BlockSpec((tm, D_MODEL), lambda i: (i, col))
    wfull = pl.BlockSpec((D_MODEL, D_MODEL), lambda i: (0, 0))
    return pl.pallas_call(
        _merge_body,
        grid=(n // tm,),
        in_specs=[tok(0), tok(0), tok(0), tok(COL_GS // D_MODEL), tok(COL_GG // D_MODEL),
                  wfull, wfull, wfull, pl.BlockSpec((1, D_MODEL), lambda i: (0, 0))],
        out_specs=[pl.BlockSpec((tm, D_MODEL), lambda i: (i, 0)),
                   pl.BlockSpec((D_MODEL, tm), lambda i: (0, i))],
        out_shape=[jax.ShapeDtypeStruct((n, D_MODEL), F32),
                   jax.ShapeDtypeStruct((D_MODEL, n), BF16)],
        compiler_params=_cparams(("parallel",)),
        name="merge",
    )(x2d, ys, o, proj, proj, wso, wgo, wo, norm_w)


NEG_INF = float("-inf")
PAIR_ROWS = ((1, 8), (2, 5), (3, 4), (4, 3), (5, 2), (6, 2), (7, 2))
N_PAIR_PAD = 30


def _count_rows(mask):
    return jnp.sum(jnp.where(mask, 1.0, 0.0), axis=0, keepdims=True)


def _top16_fast(s, want_rank):
    row = lax.broadcasted_iota(jnp.int32, (PEER_TOPK, s.shape[1]), 0)
    vals = jnp.zeros((PEER_TOPK, s.shape[1]), F32)
    rank = jnp.full(s.shape, float(PEER_TOPK), F32) if want_rank else None
    cur = s
    for it in range(PEER_TOPK):
        m = jnp.max(cur, axis=0, keepdims=True)
        vals = jnp.where(row == it, m, vals)
        eq = cur == m
        if want_rank:
            rank = jnp.where(eq, float(it), rank)
        cur = jnp.where(eq, NEG_INF, cur)
    return vals, rank, _count_rows(cur == NEG_INF)


def _top16_exact(s):
    row = lax.broadcasted_iota(jnp.int32, (PEER_TOPK, s.shape[1]), 0)
    idx = lax.broadcasted_iota(jnp.int32, s.shape, 0).astype(F32)
    vals = jnp.zeros((PEER_TOPK, s.shape[1]), F32)
    rank = jnp.full(s.shape, float(PEER_TOPK), F32)
    cur = s
    for it in range(PEER_TOPK):
        m = jnp.max(cur, axis=0, keepdims=True)
        vals = jnp.where(row == it, m, vals)
        hit = idx == jnp.min(jnp.where(cur == m, idx, float(s.shape[0])), axis=0, keepdims=True)
        rank = jnp.where(hit, float(it), rank)
        cur = jnp.where(hit, NEG_INF, cur)
    return vals, rank


def _head_fast(s1, s2):
    tt = s1.shape[1]
    row8 = lax.broadcasted_iota(jnp.int32, (SUBLANES, tt), 0)
    v1, _, gone1 = _top16_fast(s1, False)
    v2, rank2, gone2 = _top16_fast(s2, True)
    cands = [v1[0:1] + v2]
    v2lo = v2[0:SUBLANES]
    for a, nb in PAIR_ROWS:
        cands.append(jnp.where(row8 < nb, v1[a:a + 1] + v2lo, NEG_INF))
    cands.append(v1[SUBLANES:PEER_TOPK] + v2[0:1])
    cur = jnp.concatenate(cands, axis=0)
    top = v1[0:1] + v2[0:1]
    zsum = jnp.zeros((1, tt), F32)
    tau = top
    for _ in range(PEER_TOPK):
        tau = jnp.max(cur, axis=0, keepdims=True)
        zsum = zsum + jnp.exp(tau - top)
        cur = jnp.where(cur == tau, NEG_INF, cur)
    gone3 = _count_rows(cur == NEG_INF) - float(N_PAIR_PAD)
    cnt16 = jnp.zeros((PEER_TOPK, tt), F32)
    for b in range(PEER_TOPK):
        cnt16 = cnt16 + jnp.where(v1 + v2[b:b + 1] >= tau, 1.0, 0.0)
    cnt = jnp.zeros((N_KEYS, tt), F32)
    for a in range(PEER_TOPK):
        cnt = jnp.where(s1 == v1[a:a + 1], cnt16[a:a + 1], cnt)
    tables = (rank2, jnp.exp(s2 - v2[0:1]), cnt, jnp.exp(s1 - v1[0:1]) / zsum)
    return tables, jnp.maximum(jnp.maximum(gone1, gone2), gone3)


def _head_exact(s1, s2):
    tt = s1.shape[1]
    v1, rank1 = _top16_exact(s1)
    v2, rank2 = _top16_exact(s2)
    sums = jnp.concatenate([v1[a:a + 1] + v2 for a in range(PEER_TOPK)], axis=0)
    n_pairs = PEER_TOPK * PEER_TOPK
    flat = lax.broadcasted_iota(jnp.int32, (n_pairs, tt), 0).astype(F32)
    top = v1[0:1] + v2[0:1]
    zsum = jnp.zeros((1, tt), F32)
    cur = sums
    tau = top
    last = jnp.zeros((1, tt), F32)
    for _ in range(PEER_TOPK):
        tau = jnp.max(cur, axis=0, keepdims=True)
        last = jnp.min(jnp.where(cur == tau, flat, float(n_pairs)), axis=0, keepdims=True)
        zsum = zsum + jnp.exp(tau - top)
        cur = jnp.where(flat == last, NEG_INF, cur)
    taken = jnp.where(sums > tau, 1.0, jnp.where(sums == tau, jnp.where(flat <= last, 1.0, 0.0), 0.0))
    cnt = jnp.zeros((N_KEYS, tt), F32)
    for a in range(PEER_TOPK):
        cnt_a = jnp.sum(taken[a * PEER_TOPK:(a + 1) * PEER_TOPK], axis=0, keepdims=True)
        cnt = jnp.where(rank1 == float(a), cnt_a, cnt)
    return rank2, jnp.exp(s2 - v2[0:1]), cnt, jnp.exp(s1 - v1[0:1]) / zsum


def _topk_body(h2t_ref, wqt_ref, keys_ref, rk2_ref, e2_ref, cnt_ref, c_ref, s_scr, tie_scr):
    qt = _dot(wqt_ref[...], h2t_ref[...])

    def store(h, tables):
        rank2, e2, cnt, c = tables
        rk2_ref[h] = rank2.astype(BF16)
        e2_ref[h] = e2.astype(BF16)
        cnt_ref[h] = cnt
        c_ref[h] = c

    for h in range(PEER_HEADS):
        base = h * 2 * PEER_HALF
        s1 = _dot(keys_ref[h, 0], qt[base:base + PEER_HALF].astype(BF16))
        s2 = _dot(keys_ref[h, 1], qt[base + PEER_HALF:base + 2 * PEER_HALF].astype(BF16))
        tables, gone = _head_fast(s1, s2)
        store(h, tables)
        s_scr[2 * h] = s1
        s_scr[2 * h + 1] = s2
        tie_scr[h] = (jnp.max(gone) > float(PEER_TOPK)).astype(jnp.int32)

    def redo(h, carry):
        @pl.when(tie_scr[h] > 0)
        def _():
            store(h, _head_exact(s_scr[2 * h], s_scr[2 * h + 1]))
        return carry

    lax.fori_loop(0, PEER_HEADS, redo, 0)


def _topk_call(h2t, wqt, keys, tt):
    n = h2t.shape[1]
    tab = pl.BlockSpec((PEER_HEADS, N_KEYS, tt), lambda i: (0, 0, i))
    tab_f32 = jax.ShapeDtypeStruct((PEER_HEADS, N_KEYS, n), F32)
    tab_b16 = jax.ShapeDtypeStruct((PEER_HEADS, N_KEYS, n), BF16)
    return pl.pallas_call(
        _topk_body,
        grid=(n // tt,),
        in_specs=[pl.BlockSpec((D_MODEL, tt), lambda i: (0, i)),
                  pl.BlockSpec(wqt.shape, lambda i: (0, 0)),
                  pl.BlockSpec(keys.shape, lambda i: (0, 0, 0, 0))],
        out_specs=[tab, tab, tab, tab],
        out_shape=[tab_b16, tab_b16, tab_f32, tab_f32],
        scratch_shapes=[pltpu.VMEM((2 * PEER_HEADS, N_KEYS, tt), F32), pltpu.SMEM((PEER_HEADS,), jnp.int32)],
        compiler_params=_cparams(("parallel",)),
        name="topk",
    )(h2t, wqt, keys)


def _gelu(x):
    x = x.astype(BF16)
    return (0.5 * x) * (1.0 + lax.erf(x * (2.0 ** -0.5)))


def _peer_gate_mul(act_scr, w_scr, rk2_ref, e2_ref, cnt_ref, c_ref, chunk, groups):
    tt = act_scr.shape[1]
    zero = jnp.zeros((), BF16)
    key0 = pl.multiple_of(chunk * groups, groups)
    for ii in range(groups):
        gate = None
        for h in range(PEER_HEADS):
            cnt_row = cnt_ref[h, pl.ds(key0, groups), :][ii:ii + 1]
            c_row = c_ref[h, pl.ds(key0, groups), :][ii:ii + 1]
            cnt_t = jnp.broadcast_to(cnt_row, (N_KEYS, tt)).astype(BF16)
            c_t = jnp.broadcast_to(c_row, (N_KEYS, tt)).astype(BF16)
            term = jnp.where(rk2_ref[h] < cnt_t, e2_ref[h], zero) * c_t
            gate = term if gate is None else gate + term
        rows = slice(ii * N_KEYS, (ii + 1) * N_KEYS)
        w_scr[rows, :] = act_scr[rows, :] * gate


def _peer_body(h2t_ref, x1_ref, rk2_ref, e2_ref, cnt_ref, c_ref, u_ref, vt_ref, nw_ref, y_ref,
               acc_scr, act_scr, w_scr, *, ec):
    e = pl.program_id(1)
    groups = ec // N_KEYS

    @pl.when(e == 0)
    def _():
        acc_scr[...] = jnp.zeros_like(acc_scr)

    act_scr[...] = _gelu(_dot(u_ref[...], h2t_ref[...]))
    _peer_gate_mul(act_scr, w_scr, rk2_ref, e2_ref, cnt_ref, c_ref, e, groups)
    acc_scr[...] += _dot(vt_ref[...], w_scr[...])

    @pl.when(e == pl.num_programs(1) - 1)
    def _():
        x2 = x1_ref[...] + acc_scr[...].T
        ms = jnp.mean(x2 * x2, axis=-1, keepdims=True)
        y_ref[...] = x2 * lax.rsqrt(ms + EPS) * nw_ref[...]


def _peer_call(h2t, x1, rk2, e2, cnt, cc, u_b, vt_b, norm_w, tt, ec):
    n = x1.shape[0]
    assert (ec // N_KEYS) % SUBLANES == 0, "an expert chunk covers whole sublane tiles of first-half keys"
    tab = pl.BlockSpec((PEER_HEADS, N_KEYS, tt), lambda i, e: (0, 0, i))
    chunk_shape = pltpu.VMEM((ec, tt), BF16)
    return pl.pallas_call(
        functools.partial(_peer_body, ec=ec),
        grid=(n // tt, u_b.shape[0] // ec),
        in_specs=[pl.BlockSpec((D_MODEL, tt), lambda i, e: (0, i)),
                  pl.BlockSpec((tt, D_MODEL), lambda i, e: (i, 0)),
                  tab, tab, tab, tab,
                  pl.BlockSpec((ec, D_MODEL), lambda i, e: (e, 0)),
                  pl.BlockSpec((D_MODEL, ec), lambda i, e: (0, e)),
                  pl.BlockSpec((1, D_MODEL), lambda i, e: (0, 0))],
        out_specs=pl.BlockSpec((tt, D_MODEL), lambda i, e: (i, 0)),
        out_shape=jax.ShapeDtypeStruct((n, D_MODEL), F32),
        scratch_shapes=[pltpu.VMEM((D_MODEL, tt), F32), chunk_shape, chunk_shape],
        compiler_params=_cparams(("parallel", "arbitrary")),
        name="peer",
    )(h2t, x1, rk2, e2, cnt, cc, u_b, vt_b, norm_w)


def _permute_w_in(w):
    splits = [sum(IN_SIZES[:i + 1]) for i in range(len(IN_SIZES) - 1)]
    z, xbc, dt, q, k, v, glr, r, gs, gg = jnp.split(w, splits, axis=1)
    pad = jnp.zeros((D_MODEL, PROJ_COLS - COL_DTG - SSD_HEADS - GLA_GATE_RANK), w.dtype)
    return jnp.concatenate([z, v, r, gs, gg, q, k, xbc, dt, glr, pad], axis=1).astype(BF16)


def _row(v, width=None):
    v = v.reshape(1, -1).astype(F32)
    if width is not None and v.shape[1] < width:
        v = jnp.pad(v, ((0, 0), (0, width - v.shape[1])))
    return v


def _trunk(x, conv0, ssd0, gla0, wts, bg_ssd, q_ssd, bg_gla, q_gla, tm, tt, ec, proj_dtype):
    nb, length, _ = x.shape
    n = nb * length
    x2d = x.reshape(n, D_MODEL)
    proj, dtg = _proj_call(x2d, wts["norm_mix"], wts["w_in"], tm, 1280, proj_dtype)
    p3 = proj.reshape(nb, length, PROJ_COLS)
    dtg3 = dtg.reshape(nb, length, LANES)
    ys, conv_new, ssd_new = _ssd_call(p3, dtg3, conv0, ssd0, wts["conv_w"], wts["conv_b"], wts["dtb"], wts["alog"],
                                      wts["dsk"], wts["ssd_norm"], wts["eexp"], wts["eexpt"], bg_ssd, q_ssd)
    o, gla_new = _gla_call(p3, dtg3, gla0, wts["w2"], wts["b2"], wts["gla_norm"], bg_gla, q_gla)
    x1, h2t = _merge_call(x2d, ys.reshape(n, SSD_INNER), o.reshape(n, GLA_VAL), proj,
                          wts["wso"], wts["wgo"], wts["wo"], wts["norm_ffn"], tm // 2)
    rk2, e2, cnt, cc = _topk_call(h2t, wts["wqt"], wts["keys"], tt)
    y = _peer_call(h2t, x1, rk2, e2, cnt, cc, wts["u"], wts["vt"], wts["final_norm"], tt, ec)
    return y.reshape(nb, length, D_MODEL), conv_new[None], ssd_new[None], gla_new[None]


def _weights(p):
    assert p["norm_mix_w"].shape[0] == 1, "one layer"
    w2_pad = jnp.zeros((LANES, GLA_KEY), F32).at[SSD_HEADS:SSD_HEADS + GLA_GATE_RANK].set(p["w_gla_gate2"][0])
    head_of_lane = jnp.arange(SSD_INNER) // SSD_HEADDIM
    return {
        "norm_mix": _row(p["norm_mix_w"][0]),
        "w_in": _permute_w_in(p["w_in"][0]),
        "conv_w": p["conv_w"][0], "conv_b": _row(p["conv_b"][0]),
        "dtb": _row(p["dt_bias"][0], LANES), "alog": _row(p["a_log"][0], LANES),
        "dsk": _row(jnp.repeat(p["d_skip"][0], SSD_HEADDIM)),
        "ssd_norm": _row(p["ssd_norm_w"][0]),
        "eexp": (jnp.arange(LANES)[:, None] == head_of_lane[None, :]).astype(BF16),
        "eexpt": (head_of_lane[:, None] == jnp.arange(LANES)[None, :]).astype(BF16),
        "w2": w2_pad.astype(BF16), "b2": _row(p["b_gla_gate"][0]), "gla_norm": _row(p["gla_norm_w"][0]),
        "wso": p["w_ssd_out"][0].astype(BF16), "wgo": p["w_gla_out"][0].astype(BF16),
        "wo": p["w_out"][0].astype(BF16),
        "norm_ffn": _row(p["norm_ffn_w"][0]),
        "wqt": p["w_query"][0].T.astype(BF16), "keys": p["sub_keys"][0].astype(BF16),
        "u": p["expert_u"][0].astype(BF16), "vt": p["expert_v"][0].T.astype(BF16),
        "final_norm": _row(p["final_norm_w"]),
    }


def kernel(x_prompt, x_sample, state_conv, state_ssd, state_gla, norm_mix_w, w_in, conv_w, conv_b, dt_bias,
           a_log, d_skip, ssd_norm_w, w_gla_gate2, b_gla_gate, gla_norm_w, w_ssd_out, w_gla_out, w_out,
           norm_ffn_w, w_query, sub_keys, expert_u, expert_v, final_norm_w):
    wts = _weights(dict(
        norm_mix_w=norm_mix_w, w_in=w_in, conv_w=conv_w, conv_b=conv_b, dt_bias=dt_bias, a_log=a_log,
        d_skip=d_skip, ssd_norm_w=ssd_norm_w, w_gla_gate2=w_gla_gate2, b_gla_gate=b_gla_gate,
        gla_norm_w=gla_norm_w, w_ssd_out=w_ssd_out, w_gla_out=w_gla_out, w_out=w_out, norm_ffn_w=norm_ffn_w,
        w_query=w_query, sub_keys=sub_keys, expert_u=expert_u, expert_v=expert_v, final_norm_w=final_norm_w))
    bp = x_prompt.shape[0]
    conv0 = jnp.zeros((bp, CONV_W - 1, CONV_DIM), F32)
    ssd0 = jnp.zeros((bp, SSD_HEADS, SSD_HEADDIM, SSD_STATE), F32)
    gla0 = jnp.zeros((bp, GLA_HEADS, GLA_DK, GLA_DV), F32)
    ls = x_sample.shape[1]
    y_p, conv_p, ssd_p, gla_p = _trunk(x_prompt, conv0, ssd0, gla0, wts,
                                       bg_ssd=1, q_ssd=ROWS, bg_gla=2, q_gla=ROWS // 2, tm=1024, tt=512, ec=2048,
                                       proj_dtype=BF16)
    y_s, conv_s, ssd_s, gla_s = _trunk(x_sample, state_conv[0], state_ssd[0], state_gla[0], wts,
                                       bg_ssd=ROWS // ls, q_ssd=ls, bg_gla=ROWS // ls, q_gla=ls,
                                       tm=1024, tt=512, ec=2048, proj_dtype=F32)
    return (y_p, y_s, conv_p, ssd_p, gla_p, conv_s, ssd_s, gla_s)
```

```python
import functools
import math

import jax
import jax.numpy as jnp
from jax import lax
from jax.experimental import pallas as pl
from jax.experimental.pallas import tpu as pltpu

F32 = jnp.float32
BF16 = jnp.bfloat16

D_MODEL = 1024
SSD_HEADS = 16
SSD_HEADDIM = 64
SSD_INNER = 1024
SSD_GROUPS = 2
SSD_STATE = 64
CONV_W = 4
CONV_DIM = 1280
GLA_HEADS = 4
GLA_DK = 128
GLA_DV = 256
GLA_KEY = 512
GLA_VAL = 1024
GLA_GATE_RANK = 16
GLA_GATE_NORM = 16.0
IN_SIZES = (SSD_INNER, CONV_DIM, SSD_HEADS, GLA_KEY, GLA_KEY, GLA_VAL, GLA_GATE_RANK, GLA_VAL, D_MODEL, D_MODEL)
PEER_HEADS = 8
N_KEYS = 128
PEER_HALF = 128
PEER_TOPK = 16
EPS = 1e-6

LANES = 128
SUBLANES = 8
ROWS = 128
VMEM_LIMIT = 56 * 1024 * 1024

COL_Z, COL_V, COL_R, COL_GS, COL_GG = 0, 1024, 2048, 3072, 4096
COL_Q, COL_K, COL_XS, COL_BC, COL_DTG = 5120, 5632, 6144, 7168, 7424
PROJ_COLS = 7680

NT_DIMS = (((1,), (1,)), ((), ()))


def _cparams(sem):
    return pltpu.CompilerParams(dimension_semantics=sem, vmem_limit_bytes=VMEM_LIMIT)


def _dot(a, b):
    return jnp.dot(a, b, preferred_element_type=F32)


def _split3(x):
    hi = x.astype(BF16)
    r = x - hi.astype(F32)
    mid = r.astype(BF16)
    return hi, mid, (r - mid.astype(F32)).astype(BF16)


def _dot_sel(sel, x):
    hi, mid, lo = _split3(x)
    sel = sel.astype(BF16)
    return _dot(sel, hi) + (_dot(sel, mid) + _dot(sel, lo))


def _dot_sel_r(x, sel):
    hi, mid, lo = _split3(x)
    sel = sel.astype(BF16)
    return _dot(hi, sel) + (_dot(mid, sel) + _dot(lo, sel))


def _dot_nt(a, b):
    return lax.dot_general(a, b, NT_DIMS, preferred_element_type=F32)


def _silu(x):
    return x * jax.nn.sigmoid(x)


def _proj_body(x_ref, nw_ref, w_ref, o_ref, dtg_ref, h_scr, *, dtg_tile, dtg_off):
    @pl.when(pl.program_id(1) == 0)
    def _():
        x = x_ref[...]
        ms = jnp.mean(x * x, axis=-1, keepdims=True)
        h_scr[...] = (x * lax.rsqrt(ms + EPS) * nw_ref[...]).astype(BF16)

    acc = _dot(h_scr[...], w_ref[...])
    o_ref[...] = acc.astype(o_ref.dtype)

    @pl.when(pl.program_id(1) == dtg_tile)
    def _():
        dtg_ref[...] = acc[:, dtg_off:dtg_off + LANES]


def _proj_call(x2d, norm_w, w_perm, tm, tn, out_dtype):
    n = x2d.shape[0]
    dtg_tile, dtg_off = divmod(COL_DTG, tn)
    assert dtg_off + LANES <= tn
    return pl.pallas_call(
        functools.partial(_proj_body, dtg_tile=dtg_tile, dtg_off=dtg_off),
        grid=(n // tm, PROJ_COLS // tn),
        in_specs=[
            pl.BlockSpec((tm, D_MODEL), lambda i, j: (i, 0)),
            pl.BlockSpec((1, D_MODEL), lambda i, j: (0, 0)),
            pl.BlockSpec((D_MODEL, tn), lambda i, j: (0, j)),
        ],
        out_specs=[pl.BlockSpec((tm, tn), lambda i, j: (i, j)),
                   pl.BlockSpec((tm, LANES), lambda i, j: (i, 0))],
        out_shape=[jax.ShapeDtypeStruct((n, PROJ_COLS), out_dtype),
                   jax.ShapeDtypeStruct((n, LANES), F32)],
        scratch_shapes=[pltpu.VMEM((tm, D_MODEL), BF16)],
        compiler_params=_cparams(("parallel", "arbitrary")),
        name="proj",
    )(x2d, norm_w, w_perm)


def _block_masks(rows, q):
    ri = lax.broadcasted_iota(jnp.int32, (rows, rows), 0)
    ci = lax.broadcasted_iota(jnp.int32, (rows, rows), 1)
    shift = int(math.log2(q))
    same = (ri >> shift) == (ci >> shift)
    causal = same & (ci <= ri)
    return same, causal


def _row_select(rows, width, r0, q):
    rcol = lax.broadcasted_iota(jnp.int32, (rows, 1), 0)
    rowm = (rcol >= r0) & (rcol < r0 + q)
    rfull = lax.broadcasted_iota(jnp.int32, (rows, width), 0)
    sel = jnp.where(rfull == r0 + (q - 1), 1.0, 0.0).astype(F32)
    return rowm, sel


def _ssd_body(xs_ref, bc_ref, dtg_ref, z_ref, cprev_ref, h0_ref, cw_ref, cb_ref, dtb_ref, alog_ref,
              dsk_ref, nw_ref, eexp_ref, eexpt_ref, ys_ref, cnew_ref, hout_ref,
              xext, hst, c_scr, b_scr, xwt_scr, cdt_scr, yoff_scr, *, bg, q):
    rows = bg * q
    c = pl.program_id(1)
    ngroup = SSD_HEADS // SSD_GROUPS
    gw = ngroup * SSD_HEADDIM

    @pl.when(c == 0)
    def _():
        xext[:, 5:8, :] = cprev_ref[...]
        hst[...] = h0_ref[...]

    xext[:, 8:8 + q, 0:SSD_INNER] = xs_ref[...].astype(F32)
    xext[:, 8:8 + q, SSD_INNER:CONV_DIM] = bc_ref[...].astype(F32)
    acc = cb_ref[...][None]
    for k in range(CONV_W):
        acc = acc + cw_ref[k:k + 1, :][None] * xext[:, 5 + k:5 + k + q, :]
    xc = _silu(acc).reshape(rows, CONV_DIM)
    tail = xext[:, q + 5:q + 8, :]
    cnew_ref[...] = tail
    xext[:, 5:8, :] = tail

    xs = xc[:, :SSD_INNER]
    bm = xc[:, SSD_INNER:SSD_INNER + LANES]
    cm = xc[:, SSD_INNER + LANES:CONV_DIM]

    lane = lax.broadcasted_iota(jnp.int32, (1, LANES), 1)
    hmask = lane < SSD_HEADS
    dt = jnp.where(hmask, jax.nn.softplus(dtg_ref[...].reshape(rows, LANES) + dtb_ref[...]), 0.0)
    a = dt * jnp.where(hmask, -jnp.exp(alog_ref[...]), 0.0)

    same, causal = _block_masks(rows, q)
    acum = _dot_sel(causal, a)
    alast = _dot_sel(same, a)

    eexp = eexp_ref[...]
    dt_x = _dot(dt.astype(BF16), eexp)
    e_x = _dot(jnp.exp(acum).astype(BF16), eexp)
    te_x = _dot(jnp.exp(alast - acum).astype(BF16), eexp)
    xdt = xs * dt_x
    xw = xdt * te_x

    cbs = []
    for g in range(SSD_GROUPS):
        cb = _dot_nt(cm[:, g * SSD_STATE:(g + 1) * SSD_STATE].astype(BF16),
                     bm[:, g * SSD_STATE:(g + 1) * SSD_STATE].astype(BF16))
        cbs.append(jnp.where(causal, cb, 0.0))
    ydiag = []
    for h in range(SSD_HEADS):
        col = jnp.broadcast_to(acum[:, h:h + 1], (rows, rows))
        dec = jnp.exp(jnp.where(causal, col - col.T, 0.0))
        w = (cbs[h // ngroup] * dec).astype(BF16)
        ydiag.append(_dot(w, xdt[:, h * SSD_HEADDIM:(h + 1) * SSD_HEADDIM].astype(BF16)))
    y = jnp.concatenate(ydiag, axis=1)

    c_scr[...] = cm
    b_scr[...] = bm
    cdt_scr[...] = jnp.exp(alast).T
    for g in range(SSD_GROUPS):
        xwt_scr[g] = xw[:, g * gw:(g + 1) * gw].T.astype(BF16)

    def seq_step(b, carry):
        r0 = pl.multiple_of(b * q, q)
        rowm, sel = _row_select(rows, SSD_STATE, r0, q)
        for g in range(SSD_GROUPS):
            hg = hst[b, g * ngroup:(g + 1) * ngroup].reshape(gw, SSD_STATE)
            cg = c_scr[pl.ds(r0, q), g * SSD_STATE:(g + 1) * SSD_STATE]
            yoff_scr[pl.ds(r0, q), g * gw:(g + 1) * gw] = _dot_nt(cg.astype(BF16), hg.astype(BF16))
            bmask = jnp.where(rowm, b_scr[:, g * SSD_STATE:(g + 1) * SSD_STATE], 0.0).astype(BF16)
            s_new = _dot(xwt_scr[g], bmask)
            cd = _dot_sel(eexpt_ref[g * gw:(g + 1) * gw, :], _dot_sel_r(cdt_scr[...], sel))
            hst[b, g * ngroup:(g + 1) * ngroup] = (cd * hg + s_new).reshape(ngroup, SSD_HEADDIM, SSD_STATE)
        return carry

    lax.fori_loop(0, bg, seq_step, 0)

    y = y + yoff_scr[...] * e_x + dsk_ref[...] * xs
    yz = y * _silu(z_ref[...].astype(F32).reshape(rows, SSD_INNER))
    ms = jnp.mean(yz * yz, axis=-1, keepdims=True)
    ys_ref[...] = (yz * lax.rsqrt(ms + EPS) * nw_ref[...]).astype(BF16).reshape(bg, q, SSD_INNER)

    @pl.when(c == pl.num_programs(1) - 1)
    def _():
        hout_ref[...] = hst[...]


def _ssd_call(p3, dtg3, conv_prev, h0, conv_w, conv_b, dtb, alog, dsk_x, norm_w, eexp, eexpt, bg, q):
    nb, length, _ = p3.shape
    full = lambda shape: pl.BlockSpec(shape, lambda i, c: (0,) * len(shape))
    rows = bg * q
    gw = SSD_INNER // SSD_GROUPS
    return pl.pallas_call(
        functools.partial(_ssd_body, bg=bg, q=q),
        grid=(nb // bg, length // q),
        in_specs=[
            pl.BlockSpec((bg, q, SSD_INNER), lambda i, c: (i, c, COL_XS // SSD_INNER)),
            pl.BlockSpec((bg, q, 2 * LANES), lambda i, c: (i, c, COL_BC // (2 * LANES))),
            pl.BlockSpec((bg, q, LANES), lambda i, c: (i, c, 0)),
            pl.BlockSpec((bg, q, SSD_INNER), lambda i, c: (i, c, COL_Z // SSD_INNER)),
            pl.BlockSpec((bg, CONV_W - 1, CONV_DIM), lambda i, c: (i, 0, 0)),
            pl.BlockSpec((bg, SSD_HEADS, SSD_HEADDIM, SSD_STATE), lambda i, c: (i, 0, 0, 0)),
            full((CONV_W, CONV_DIM)), full((1, CONV_DIM)), full((1, LANES)), full((1, LANES)),
            full((1, SSD_INNER)), full((1, SSD_INNER)), full((LANES, SSD_INNER)), full((SSD_INNER, LANES)),
        ],
        out_specs=[
            pl.BlockSpec((bg, q, SSD_INNER), lambda i, c: (i, c, 0)),
            pl.BlockSpec((bg, CONV_W - 1, CONV_DIM), lambda i, c: (i, 0, 0)),
            pl.BlockSpec((bg, SSD_HEADS, SSD_HEADDIM, SSD_STATE), lambda i, c: (i, 0, 0, 0)),
        ],
        out_shape=[
            jax.ShapeDtypeStruct((nb, length, SSD_INNER), BF16),
            jax.ShapeDtypeStruct((nb, CONV_W - 1, CONV_DIM), F32),
            jax.ShapeDtypeStruct((nb, SSD_HEADS, SSD_HEADDIM, SSD_STATE), F32),
        ],
        scratch_shapes=[
            pltpu.VMEM((bg, q + 8, CONV_DIM), F32),
            pltpu.VMEM((bg, SSD_HEADS, SSD_HEADDIM, SSD_STATE), F32),
            pltpu.VMEM((rows, LANES), F32),
            pltpu.VMEM((rows, LANES), F32),
            pltpu.VMEM((SSD_GROUPS, gw, rows), BF16),
            pltpu.VMEM((LANES, rows), F32),
            pltpu.VMEM((rows, SSD_INNER), F32),
        ],
        compiler_params=_cparams(("parallel", "arbitrary")),
        name="ssd",
    )(p3, p3, dtg3, p3, conv_prev, h0, conv_w, conv_b, dtb, alog, dsk_x, norm_w, eexp, eexpt)


def _gla_body(q_ref, k_ref, v_ref, r_ref, dtg_ref, s0_ref, w2_ref, b2_ref, nw_ref, o_ref, sout_ref,
              sst, qe_scr, kdt_scr, v_scr, cdt_scr, oint_scr, *, bg, q):
    rows = bg * q
    c = pl.program_id(1)

    @pl.when(c == 0)
    def _():
        sst[...] = s0_ref[...]

    qv = q_ref[...].astype(F32).reshape(rows, GLA_KEY)
    kv = k_ref[...].astype(F32).reshape(rows, GLA_KEY)
    vv = v_ref[...].astype(F32).reshape(rows, GLA_VAL)
    glr = dtg_ref[...].reshape(rows, LANES)
    g = jax.nn.log_sigmoid(_dot(glr.astype(BF16), w2_ref[...]) + b2_ref[...]) * (1.0 / GLA_GATE_NORM)

    same, causal = _block_masks(rows, q)
    gc = _dot_sel(causal, g)
    gl = _dot_sel(same, g)
    qe = qv * (GLA_DK ** -0.5) * jnp.exp(gc)
    ke = kv * jnp.exp(-gc)
    kd = kv * jnp.exp(gl - gc)
    cdk = jnp.exp(gl)

    qe_b = qe.astype(BF16)
    ke_b = ke.astype(BF16)
    v_b = vv.astype(BF16)
    o_intra = []
    for h in range(GLA_HEADS):
        ks = slice(h * GLA_DK, (h + 1) * GLA_DK)
        att = jnp.where(causal, _dot_nt(qe_b[:, ks], ke_b[:, ks]), 0.0)
        o_intra.append(_dot(att.astype(BF16), v_b[:, h * GLA_DV:(h + 1) * GLA_DV]))
        kdt_scr[h] = kd[:, ks].T.astype(BF16)
        cdt_scr[h] = cdk[:, ks].T
    qe_scr[...] = qe
    v_scr[...] = vv

    def seq_step(b, carry):
        r0 = pl.multiple_of(b * q, q)
        rowm, sel = _row_select(rows, LANES, r0, q)
        for h in range(GLA_HEADS):
            s_old = sst[b, h]
            qeb = qe_scr[pl.ds(r0, q), h * GLA_DK:(h + 1) * GLA_DK]
            oint_scr[pl.ds(r0, q), h * GLA_DV:(h + 1) * GLA_DV] = _dot(qeb.astype(BF16), s_old.astype(BF16))
            vm = jnp.where(rowm, v_scr[:, h * GLA_DV:(h + 1) * GLA_DV], 0.0).astype(BF16)
            s_new = _dot(kdt_scr[h], vm)
            cd = _dot_sel_r(cdt_scr[h], sel)
            sst[b, h] = jnp.concatenate([cd] * (GLA_DV // LANES), axis=1) * s_old + s_new
        return carry

    lax.fori_loop(0, bg, seq_step, 0)

    rv = r_ref[...].astype(F32).reshape(rows, GLA_VAL)
    outs = []
    for h in range(GLA_HEADS):
        vs = slice(h * GLA_DV, (h + 1) * GLA_DV)
        o = o_intra[h] + oint_scr[:, vs]
        ms = jnp.mean(o * o, axis=-1, keepdims=True)
        outs.append(o * lax.rsqrt(ms + EPS) * nw_ref[...])
    o_all = jnp.concatenate(outs, axis=1) * _silu(rv)
    o_ref[...] = o_all.astype(BF16).reshape(bg, q, GLA_VAL)

    @pl.when(c == pl.num_programs(1) - 1)
    def _():
        sout_ref[...] = sst[...]


def _gla_call(p3, dtg3, s0, w2_pad, b2, norm_w, bg, q):
    nb, length, _ = p3.shape
    full = lambda shape: pl.BlockSpec(shape, lambda i, c: (0,) * len(shape))
    rows = bg * q
    state_block = (bg, GLA_HEADS, GLA_DK, GLA_DV)
    return pl.pallas_call(
        functools.partial(_gla_body, bg=bg, q=q),
        grid=(nb // bg, length // q),
        in_specs=[
            pl.BlockSpec((bg, q, GLA_KEY), lambda i, c: (i, c, COL_Q // GLA_KEY)),
            pl.BlockSpec((bg, q, GLA_KEY), lambda i, c: (i, c, COL_K // GLA_KEY)),
            pl.BlockSpec((bg, q, GLA_VAL), lambda i, c: (i, c, COL_V // GLA_VAL)),
            pl.BlockSpec((bg, q, GLA_VAL), lambda i, c: (i, c, COL_R // GLA_VAL)),
            pl.BlockSpec((bg, q, LANES), lambda i, c: (i, c, 0)),
            pl.BlockSpec(state_block, lambda i, c: (i, 0, 0, 0)),
            full((LANES, GLA_KEY)), full((1, GLA_KEY)), full((1, GLA_DV)),
        ],
        out_specs=[
            pl.BlockSpec((bg, q, GLA_VAL), lambda i, c: (i, c, 0)),
            pl.BlockSpec(state_block, lambda i, c: (i, 0, 0, 0)),
        ],
        out_shape=[
            jax.ShapeDtypeStruct((nb, length, GLA_VAL), BF16),
            jax.ShapeDtypeStruct((nb, GLA_HEADS, GLA_DK, GLA_DV), F32),
        ],
        scratch_shapes=[
            pltpu.VMEM(state_block, F32),
            pltpu.VMEM((rows, GLA_KEY), F32),
            pltpu.VMEM((GLA_HEADS, GLA_DK, rows), BF16),
            pltpu.VMEM((rows, GLA_VAL), F32),
            pltpu.VMEM((GLA_HEADS, GLA_DK, rows), F32),
            pltpu.VMEM((rows, GLA_VAL), F32),
        ],
        compiler_params=_cparams(("parallel", "arbitrary")),
        name="gla",
    )(p3, p3, p3, p3, dtg3, s0, w2_pad, b2, norm_w)


def _merge_body(x_ref, ys_ref, o_ref, gs_ref, gg_ref, wso_ref, wgo_ref, wo_ref, nw_ref, x1_ref, h2t_ref):
    mix = (jax.nn.sigmoid(gs_ref[...].astype(F32)) * _dot(ys_ref[...], wso_ref[...])
           + jax.nn.sigmoid(gg_ref[...].astype(F32)) * _dot(o_ref[...], wgo_ref[...]))
    x1 = x_ref[...] + _dot(mix.astype(BF16), wo_ref[...])
    x1_ref[...] = x1
    ms = jnp.mean(x1 * x1, axis=-1, keepdims=True)
    h2 = x1 * lax.rsqrt(ms + EPS) * nw_ref[...]
    h2t_ref[...] = h2.T.astype(BF16)


def _merge_call(x2d, ys, o, proj, wso, wgo, wo, norm_w, tm):
    n = x2d.shape[0]
    tok = lambda col: pl.BlockSpec((tm, D_MODEL), lambda i: (i, col))
    wfull = pl.BlockSpec((D_MODEL, D_MODEL), lambda i: (0, 0))
    return pl.pallas_call(
        _merge_body,
        grid=(n // tm,),
        in_specs=[tok(0), tok(0), tok(0), tok(COL_GS // D_MODEL), tok(COL_GG // D_MODEL),
                  wfull, wfull, wfull, pl.BlockSpec((1, D_MODEL), lambda i: (0, 0))],
        out_specs=[pl.BlockSpec((tm, D_MODEL), lambda i: (i, 0)),
                   pl.BlockSpec((D_MODEL, tm), lambda i: (0, i))],
        out_shape=[jax.ShapeDtypeStruct((n, D_MODEL), F32),
                   jax.ShapeDtypeStruct((D_MODEL, n), BF16)],
        compiler_params=_cparams(("parallel",)),
        name="merge",
    )(x2d, ys, o, proj, proj, wso, wgo, wo, norm_w)


NEG_INF = float("-inf")
PAIR_ROWS = ((1, 8), (2, 5), (3, 4), (4, 3), (5, 2), (6, 2), (7, 2))
N_PAIR_PAD = 30


def _count_rows(mask):
    return jnp.sum(jnp.where(mask, 1.0, 0.0), axis=0, keepdims=True)


def _top16_fast(s, want_rank):
    row = lax.broadcasted_iota(jnp.int32, (PEER_TOPK, s.shape[1]), 0)
    vals = jnp.zeros((PEER_TOPK, s.shape[1]), F32)
    rank = jnp.full(s.shape, float(PEER_TOPK), F32) if want_rank else None
    cur = s
    for it in range(PEER_TOPK):
        m = jnp.max(cur, axis=0, keepdims=True)
        vals = jnp.where(row == it, m, vals)
        eq = cur == m
        if want_rank:
            rank = jnp.where(eq, float(it), rank)
        cur = jnp.where(eq, NEG_INF, cur)
    return vals, rank, _count_rows(cur == NEG_INF)


def _top16_exact(s):
    row = lax.broadcasted_iota(jnp.int32, (PEER_TOPK, s.shape[1]), 0)
    idx = lax.broadcasted_iota(jnp.int32, s.shape, 0).astype(F32)
    vals = jnp.zeros((PEER_TOPK, s.shape[1]), F32)
    rank = jnp.full(s.shape, float(PEER_TOPK), F32)
    cur = s
    for it in range(PEER_TOPK):
        m = jnp.max(cur, axis=0, keepdims=True)
        vals = jnp.where(row == it, m, vals)
        hit = idx == jnp.min(jnp.where(cur == m, idx, float(s.shape[0])), axis=0, keepdims=True)
        rank = jnp.where(hit, float(it), rank)
        cur = jnp.where(hit, NEG_INF, cur)
    return vals, rank


def _head_fast(s1, s2, pair_ties=False):
    tt = s1.shape[1]
    row8 = lax.broadcasted_iota(jnp.int32, (SUBLANES, tt), 0)
    v1, _, gone1 = _top16_fast(s1, False)
    v2, rank2, gone2 = _top16_fast(s2, True)
    cands = [v1[0:1] + v2]
    firsts = [jnp.broadcast_to(v1[0:1], (PEER_TOPK, tt))]
    v2lo = v2[0:SUBLANES]
    for a, nb in PAIR_ROWS:
        cands.append(jnp.where(row8 < nb, v1[a:a + 1] + v2lo, NEG_INF))
        firsts.append(jnp.broadcast_to(v1[a:a + 1], (SUBLANES, tt)))
    cands.append(v1[SUBLANES:PEER_TOPK] + v2[0:1])
    firsts.append(v1[SUBLANES:PEER_TOPK])
    cur = jnp.concatenate(cands, axis=0)
    top = v1[0:1] + v2[0:1]
    zsum = jnp.zeros((1, tt), F32)
    tau = top
    cnt16 = jnp.zeros((PEER_TOPK, tt), F32)
    if pair_ties:
        first = jnp.concatenate(firsts, axis=0)
        tau1 = v1[0:1]
        for _ in range(PEER_TOPK):
            tau = jnp.max(cur, axis=0, keepdims=True)
            eq = cur == tau
            tau1 = jnp.max(jnp.where(eq, first, NEG_INF), axis=0, keepdims=True)
            zsum = zsum + jnp.exp(tau - top)
            cur = jnp.where(eq, jnp.where(first == tau1, NEG_INF, cur), cur)
        gone3 = jnp.full((1, tt), float(PEER_TOPK), F32)
        last_ok = jnp.where(v1 >= tau1, 1.0, 0.0)
        for b in range(PEER_TOPK):
            pair = v1 + v2[b:b + 1]
            cnt16 = cnt16 + jnp.where(pair > tau, 1.0, jnp.where(pair == tau, last_ok, 0.0))
    else:
        for _ in range(PEER_TOPK):
            tau = jnp.max(cur, axis=0, keepdims=True)
            zsum = zsum + jnp.exp(tau - top)
            cur = jnp.where(cur == tau, NEG_INF, cur)
        gone3 = _count_rows(cur == NEG_INF) - float(N_PAIR_PAD)
        for b in range(PEER_TOPK):
            cnt16 = cnt16 + jnp.where(v1 + v2[b:b + 1] >= tau, 1.0, 0.0)
    cnt = jnp.zeros((N_KEYS, tt), F32)
    for a in range(PEER_TOPK):
        cnt = jnp.where(s1 == v1[a:a + 1], cnt16[a:a + 1], cnt)
    tables = (rank2, jnp.exp(s2 - v2[0:1]), cnt, jnp.exp(s1 - v1[0:1]) / zsum)
    return tables, jnp.maximum(gone1, gone2), gone3


def _head_exact(s1, s2):
    tt = s1.shape[1]
    v1, rank1 = _top16_exact(s1)
    v2, rank2 = _top16_exact(s2)
    sums = jnp.concatenate([v1[a:a + 1] + v2 for a in range(PEER_TOPK)], axis=0)
    n_pairs = PEER_TOPK * PEER_TOPK
    flat = lax.broadcasted_iota(jnp.int32, (n_pairs, tt), 0).astype(F32)
    top = v1[0:1] + v2[0:1]
    zsum = jnp.zeros((1, tt), F32)
    cur = sums
    tau = top
    last = jnp.zeros((1, tt), F32)
    for _ in range(PEER_TOPK):
        tau = jnp.max(cur, axis=0, keepdims=True)
        last = jnp.min(jnp.where(cur == tau, flat, float(n_pairs)), axis=0, keepdims=True)
        zsum = zsum + jnp.exp(tau - top)
        cur = jnp.where(flat == last, NEG_INF, cur)
    taken = jnp.where(sums > tau, 1.0, jnp.where(sums == tau, jnp.where(flat <= last, 1.0, 0.0), 0.0))
    cnt = jnp.zeros((N_KEYS, tt), F32)
    for a in range(PEER_TOPK):
        cnt_a = jnp.sum(taken[a * PEER_TOPK:(a + 1) * PEER_TOPK], axis=0, keepdims=True)
        cnt = jnp.where(rank1 == float(a), cnt_a, cnt)
    return rank2, jnp.exp(s2 - v2[0:1]), cnt, jnp.exp(s1 - v1[0:1]) / zsum


def _topk_body(h2t_ref, wqt_ref, keys_ref, rk2_ref, e2_ref, cnt_ref, c_ref, s_scr, tie_scr):
    qt = _dot(wqt_ref[...], h2t_ref[...])

    def store(h, tables):
        rank2, e2, cnt, c = tables
        rk2_ref[h] = rank2.astype(BF16)
        e2_ref[h] = e2.astype(BF16)
        cnt_ref[h] = cnt
        c_ref[h] = c

    for h in range(PEER_HEADS):
        base = h * 2 * PEER_HALF
        s1 = _dot(keys_ref[h, 0], qt[base:base + PEER_HALF].astype(BF16))
        s2 = _dot(keys_ref[h, 1], qt[base + PEER_HALF:base + 2 * PEER_HALF].astype(BF16))
        tables, gone_half, gone_pair = _head_fast(s1, s2)
        store(h, tables)
        s_scr[2 * h] = s1
        s_scr[2 * h + 1] = s2
        half_tie = (jnp.max(gone_half) > float(PEER_TOPK)).astype(jnp.int32)
        pair_tie = (jnp.max(gone_pair) > float(PEER_TOPK)).astype(jnp.int32)
        tie_scr[h] = jnp.maximum(2 * half_tie, pair_tie)

    def redo(h, carry):
        @pl.when(tie_scr[h] == 2)
        def _():
            store(h, _head_exact(s_scr[2 * h], s_scr[2 * h + 1]))

        @pl.when(tie_scr[h] == 1)
        def _():
            store(h, _head_fast(s_scr[2 * h], s_scr[2 * h + 1], pair_ties=True)[0])
        return carry

    lax.fori_loop(0, PEER_HEADS, redo, 0)


def _topk_call(h2t, wqt, keys, tt):
    n = h2t.shape[1]
    tab = pl.BlockSpec((PEER_HEADS, N_KEYS, tt), lambda i: (0, 0, i))
    tab_f32 = jax.ShapeDtypeStruct((PEER_HEADS, N_KEYS, n), F32)
    tab_b16 = jax.ShapeDtypeStruct((PEER_HEADS, N_KEYS, n), BF16)
    return pl.pallas_call(
        _topk_body,
        grid=(n // tt,),
        in_specs=[pl.BlockSpec((D_MODEL, tt), lambda i: (0, i)),
                  pl.BlockSpec(wqt.shape, lambda i: (0, 0)),
                  pl.BlockSpec(keys.shape, lambda i: (0, 0, 0, 0))],
        out_specs=[tab, tab, tab, tab],
        out_shape=[tab_b16, tab_b16, tab_f32, tab_f32],
        scratch_shapes=[pltpu.VMEM((2 * PEER_HEADS, N_KEYS, tt), F32), pltpu.SMEM((PEER_HEADS,), jnp.int32)],
        compiler_params=_cparams(("parallel",)),
        name="topk",
    )(h2t, wqt, keys)


def _gelu(x):
    x = x.astype(BF16)
    return (0.5 * x) * (1.0 + lax.erf(x * (2.0 ** -0.5)))


def _peer_gate_mul(act_scr, w_scr, rk2_ref, e2_ref, cnt_ref, c_ref, chunk, groups):
    tt = act_scr.shape[1]
    zero = jnp.zeros((), BF16)
    key0 = pl.multiple_of(chunk * groups, groups)
    for ii in range(groups):
        gate = None
        for h in range(PEER_HEADS):
            cnt_row = cnt_ref[h, pl.ds(key0, groups), :][ii:ii + 1]
            c_row = c_ref[h, pl.ds(key0, groups), :][ii:ii + 1]
            cnt_t = jnp.broadcast_to(cnt_row, (N_KEYS, tt)).astype(BF16)
            c_t = jnp.broadcast_to(c_row, (N_KEYS, tt)).astype(BF16)
            term = jnp.where(rk2_ref[h] < cnt_t, e2_ref[h], zero) * c_t
            gate = term if gate is None else gate + term
        rows = slice(ii * N_KEYS, (ii + 1) * N_KEYS)
        w_scr[rows, :] = act_scr[rows, :] * gate


def _peer_body(h2t_ref, x1_ref, rk2_ref, e2_ref, cnt_ref, c_ref, u_ref, vt_ref, nw_ref, y_ref,
               acc_scr, act_scr, w_scr, *, ec):
    e = pl.program_id(1)
    groups = ec // N_KEYS

    @pl.when(e == 0)
    def _():
        acc_scr[...] = jnp.zeros_like(acc_scr)

    act_scr[...] = _gelu(_dot(u_ref[...], h2t_ref[...]))
    _peer_gate_mul(act_scr, w_scr, rk2_ref, e2_ref, cnt_ref, c_ref, e, groups)
    acc_scr[...] += _dot(vt_ref[...], w_scr[...])

    @pl.when(e == pl.num_programs(1) - 1)
    def _():
        x2 = x1_ref[...] + acc_scr[...].T
        ms = jnp.mean(x2 * x2, axis=-1, keepdims=True)
        y_ref[...] = x2 * lax.rsqrt(ms + EPS) * nw_ref[...]


def _peer_call(h2t, x1, rk2, e2, cnt, cc, u_b, vt_b, norm_w, tt, ec):
    n = x1.shape[0]
    assert (ec // N_KEYS) % SUBLANES == 0, "an expert chunk covers whole sublane tiles of first-half keys"
    tab = pl.BlockSpec((PEER_HEADS, N_KEYS, tt), lambda i, e: (0, 0, i))
    chunk_shape = pltpu.VMEM((ec, tt), BF16)
    return pl.pallas_call(
        functools.partial(_peer_body, ec=ec),
        grid=(n // tt, u_b.shape[0] // ec),
        in_specs=[pl.BlockSpec((D_MODEL, tt), lambda i, e: (0, i)),
                  pl.BlockSpec((tt, D_MODEL), lambda i, e: (i, 0)),
                  tab, tab, tab, tab,
                  pl.BlockSpec((ec, D_MODEL), lambda i, e: (e, 0)),
                  pl.BlockSpec((D_MODEL, ec), lambda i, e: (0, e)),
                  pl.BlockSpec((1, D_MODEL), lambda i, e: (0, 0))],
        out_specs=pl.BlockSpec((tt, D_MODEL), lambda i, e: (i, 0)),
        out_shape=jax.ShapeDtypeStruct((n, D_MODEL), F32),
        scratch_shapes=[pltpu.VMEM((D_MODEL, tt), F32), chunk_shape, chunk_shape],
        compiler_params=_cparams(("parallel", "arbitrary")),
        name="peer",
    )(h2t, x1, rk2, e2, cnt, cc, u_b, vt_b, norm_w)


def _permute_w_in(w):
    splits = [sum(IN_SIZES[:i + 1]) for i in range(len(IN_SIZES) - 1)]
    z, xbc, dt, q, k, v, glr, r, gs, gg = jnp.split(w, splits, axis=1)
    pad = jnp.zeros((D_MODEL, PROJ_COLS - COL_DTG - SSD_HEADS - GLA_GATE_RANK), w.dtype)
    return jnp.concatenate([z, v, r, gs, gg, q, k, xbc, dt, glr, pad], axis=1).astype(BF16)


def _row(v, width=None):
    v = v.reshape(1, -1).astype(F32)
    if width is not None and v.shape[1] < width:
        v = jnp.pad(v, ((0, 0), (0, width - v.shape[1])))
    return v


def _trunk(x, conv0, ssd0, gla0, wts, bg_ssd, q_ssd, bg_gla, q_gla, tm, tt, ec, proj_dtype):
    nb, length, _ = x.shape
    n = nb * length
    x2d = x.reshape(n, D_MODEL)
    proj, dtg = _proj_call(x2d, wts["norm_mix"], wts["w_in"], tm, PROJ_COLS // 3, proj_dtype)
    p3 = proj.reshape(nb, length, PROJ_COLS)
    dtg3 = dtg.reshape(nb, length, LANES)
    ys, conv_new, ssd_new = _ssd_call(p3, dtg3, conv0, ssd0, wts["conv_w"], wts["conv_b"], wts["dtb"], wts["alog"],
                                      wts["dsk"], wts["ssd_norm"], wts["eexp"], wts["eexpt"], bg_ssd, q_ssd)
    o, gla_new = _gla_call(p3, dtg3, gla0, wts["w2"], wts["b2"], wts["gla_norm"], bg_gla, q_gla)
    x1, h2t = _merge_call(x2d, ys.reshape(n, SSD_INNER), o.reshape(n, GLA_VAL), proj,
                          wts["wso"], wts["wgo"], wts["wo"], wts["norm_ffn"], tm // 2)
    rk2, e2, cnt, cc = _topk_call(h2t, wts["wqt"], wts["keys"], tt)
    y = _peer_call(h2t, x1, rk2, e2, cnt, cc, wts["u"], wts["vt"], wts["final_norm"], tt, ec)
    return y.reshape(nb, length, D_MODEL), conv_new[None], ssd_new[None], gla_new[None]


def _weights(p):
    assert p["norm_mix_w"].shape[0] == 1, "one layer"
    w2_pad = jnp.zeros((LANES, GLA_KEY), F32).at[SSD_HEADS:SSD_HEADS + GLA_GATE_RANK].set(p["w_gla_gate2"][0])
    head_of_lane = jnp.arange(SSD_INNER) // SSD_HEADDIM
    return {
        "norm_mix": _row(p["norm_mix_w"][0]),
        "w_in": _permute_w_in(p["w_in"][0]),
        "conv_w": p["conv_w"][0], "conv_b": _row(p["conv_b"][0]),
        "dtb": _row(p["dt_bias"][0], LANES), "alog": _row(p["a_log"][0], LANES),
        "dsk": _row(jnp.repeat(p["d_skip"][0], SSD_HEADDIM)),
        "ssd_norm": _row(p["ssd_norm_w"][0]),
        "eexp": (jnp.arange(LANES)[:, None] == head_of_lane[None, :]).astype(BF16),
        "eexpt": (head_of_lane[:, None] == jnp.arange(LANES)[None, :]).astype(BF16),
        "w2": w2_pad.astype(BF16), "b2": _row(p["b_gla_gate"][0]), "gla_norm": _row(p["gla_norm_w"][0]),
        "wso": p["w_ssd_out"][0].astype(BF16), "wgo": p["w_gla_out"][0].astype(BF16),
        "wo": p["w_out"][0].astype(BF16),
        "norm_ffn": _row(p["norm_ffn_w"][0]),
        "wqt": p["w_query"][0].T.astype(BF16), "keys": p["sub_keys"][0].astype(BF16),
        "u": p["expert_u"][0].astype(BF16), "vt": p["expert_v"][0].T.astype(BF16),
        "final_norm": _row(p["final_norm_w"]),
    }


def kernel(x_prompt, x_sample, state_conv, state_ssd, state_gla, norm_mix_w, w_in, conv_w, conv_b, dt_bias,
           a_log, d_skip, ssd_norm_w, w_gla_gate2, b_gla_gate, gla_norm_w, w_ssd_out, w_gla_out, w_out,
           norm_ffn_w, w_query, sub_keys, expert_u, expert_v, final_norm_w):
    wts = _weights(dict(
        norm_mix_w=norm_mix_w, w_in=w_in, conv_w=conv_w, conv_b=conv_b, dt_bias=dt_bias, a_log=a_log,
        d_skip=d_skip, ssd_norm_w=ssd_norm_w, w_gla_gate2=w_gla_gate2, b_gla_gate=b_gla_gate,
        gla_norm_w=gla_norm_w, w_ssd_out=w_ssd_out, w_gla_out=w_gla_out, w_out=w_out, norm_ffn_w=norm_ffn_w,
        w_query=w_query, sub_keys=sub_keys, expert_u=expert_u, expert_v=expert_v, final_norm_w=final_norm_w))
    bp = x_prompt.shape[0]
    conv0 = jnp.zeros((bp, CONV_W - 1, CONV_DIM), F32)
    ssd0 = jnp.zeros((bp, SSD_HEADS, SSD_HEADDIM, SSD_STATE), F32)
    gla0 = jnp.zeros((bp, GLA_HEADS, GLA_DK, GLA_DV), F32)
    ls = x_sample.shape[1]
    y_p, conv_p, ssd_p, gla_p = _trunk(x_prompt, conv0, ssd0, gla0, wts,
                                       bg_ssd=1, q_ssd=ROWS, bg_gla=2, q_gla=ROWS // 2, tm=1024, tt=512, ec=2048,
                                       proj_dtype=BF16)
    y_s, conv_s, ssd_s, gla_s = _trunk(x_sample, state_conv[0], state_ssd[0], state_gla[0], wts,
                                       bg_ssd=ROWS // ls, q_ssd=ls, bg_gla=ROWS // ls, q_gla=ls,
                                       tm=1024, tt=512, ec=2048, proj_dtype=F32)
    return (y_p, y_s, conv_p, ssd_p, gla_p, conv_s, ssd_s, gla_s)
```

```python
import functools
import math

import jax
import jax.numpy as jnp
from jax import lax
from jax.experimental import pallas as pl
from jax.experimental.pallas import tpu as pltpu

F32 = jnp.float32
BF16 = jnp.bfloat16

D_MODEL = 1024
SSD_HEADS = 16
SSD_HEADDIM = 64
SSD_INNER = 1024
SSD_GROUPS = 2
SSD_STATE = 64
CONV_W = 4
CONV_DIM = 1280
GLA_HEADS = 4
GLA_DK = 128
GLA_DV = 256
GLA_KEY = 512
GLA_VAL = 1024
GLA_GATE_RANK = 16
GLA_GATE_NORM = 16.0
IN_SIZES = (SSD_INNER, CONV_DIM, SSD_HEADS, GLA_KEY, GLA_KEY, GLA_VAL, GLA_GATE_RANK, GLA_VAL, D_MODEL, D_MODEL)
PEER_HEADS = 8
N_KEYS = 128
PEER_HALF = 128
PEER_TOPK = 16
EPS = 1e-6

LANES = 128
SUBLANES = 8
ROWS = 128
VMEM_LIMIT = 56 * 1024 * 1024

COL_Z, COL_V, COL_R, COL_GS, COL_GG = 0, 1024, 2048, 3072, 4096
COL_Q, COL_K, COL_XS, COL_BC, COL_DTG = 5120, 5632, 6144, 7168, 7424
PROJ_COLS = 7680

NT_DIMS = (((1,), (1,)), ((), ()))


def _cparams(sem):
    return pltpu.CompilerParams(dimension_semantics=sem, vmem_limit_bytes=VMEM_LIMIT)


def _dot(a, b):
    return jnp.dot(a, b, preferred_element_type=F32)


def _split3(x):
    hi = x.astype(BF16)
    r = x - hi.astype(F32)
    mid = r.astype(BF16)
    return hi, mid, (r - mid.astype(F32)).astype(BF16)


def _dot_sel(sel, x):
    hi, mid, lo = _split3(x)
    sel = sel.astype(BF16)
    return _dot(sel, hi) + (_dot(sel, mid) + _dot(sel, lo))


def _dot_sel_r(x, sel):
    hi, mid, lo = _split3(x)
    sel = sel.astype(BF16)
    return _dot(hi, sel) + (_dot(mid, sel) + _dot(lo, sel))


def _dot_nt(a, b):
    return lax.dot_general(a, b, NT_DIMS, preferred_element_type=F32)


def _silu(x):
    return x * jax.nn.sigmoid(x)


def _proj_body(x_ref, nw_ref, w_ref, o_ref, dtg_ref, h_scr, *, dtg_tile, dtg_off):
    @pl.when(pl.program_id(1) == 0)
    def _():
        x = x_ref[...]
        ms = jnp.mean(x * x, axis=-1, keepdims=True)
        h_scr[...] = (x * lax.rsqrt(ms + EPS) * nw_ref[...]).astype(BF16)

    acc = _dot(h_scr[...], w_ref[...])
    o_ref[...] = acc.astype(o_ref.dtype)

    @pl.when(pl.program_id(1) == dtg_tile)
    def _():
        dtg_ref[...] = acc[:, dtg_off:dtg_off + LANES]


def _proj_call(x2d, norm_w, w_perm, tm, tn, out_dtype):
    n = x2d.shape[0]
    dtg_tile, dtg_off = divmod(COL_DTG, tn)
    assert dtg_off + LANES <= tn
    return pl.pallas_call(
        functools.partial(_proj_body, dtg_tile=dtg_tile, dtg_off=dtg_off),
        grid=(n // tm, PROJ_COLS // tn),
        in_specs=[
            pl.BlockSpec((tm, D_MODEL), lambda i, j: (i, 0)),
            pl.BlockSpec((1, D_MODEL), lambda i, j: (0, 0)),
            pl.BlockSpec((D_MODEL, tn), lambda i, j: (0, j)),
        ],
        out_specs=[pl.BlockSpec((tm, tn), lambda i, j: (i, j)),
                   pl.BlockSpec((tm, LANES), lambda i, j: (i, 0))],
        out_shape=[jax.ShapeDtypeStruct((n, PROJ_COLS), out_dtype),
                   jax.ShapeDtypeStruct((n, LANES), F32)],
        scratch_shapes=[pltpu.VMEM((tm, D_MODEL), BF16)],
        compiler_params=_cparams(("parallel", "arbitrary")),
        name="proj",
    )(x2d, norm_w, w_perm)


def _block_masks(rows, q):
    ri = lax.broadcasted_iota(jnp.int32, (rows, rows), 0)
    ci = lax.broadcasted_iota(jnp.int32, (rows, rows), 1)
    shift = int(math.log2(q))
    same = (ri >> shift) == (ci >> shift)
    causal = same & (ci <= ri)
    return same, causal


def _row_select(rows, width, r0, q):
    rcol = lax.broadcasted_iota(jnp.int32, (rows, 1), 0)
    rowm = (rcol >= r0) & (rcol < r0 + q)
    rfull = lax.broadcasted_iota(jnp.int32, (rows, width), 0)
    sel = jnp.where(rfull == r0 + (q - 1), 1.0, 0.0).astype(F32)
    return rowm, sel


def _ssd_body(xs_ref, bc_ref, dtg_ref, z_ref, cprev_ref, h0_ref, cw_ref, cb_ref, dtb_ref, alog_ref,
              dsk_ref, nw_ref, eexp_ref, eexpt_ref, ys_ref, cnew_ref, hout_ref,
              xext, hst, c_scr, b_scr, xwt_scr, cdt_scr, yoff_scr, *, bg, q):
    rows = bg * q
    c = pl.program_id(1)
    ngroup = SSD_HEADS // SSD_GROUPS
    gw = ngroup * SSD_HEADDIM

    @pl.when(c == 0)
    def _():
        xext[:, 5:8, :] = cprev_ref[...]
        hst[...] = h0_ref[...]

    xext[:, 8:8 + q, 0:SSD_INNER] = xs_ref[...].astype(F32)
    xext[:, 8:8 + q, SSD_INNER:CONV_DIM] = bc_ref[...].astype(F32)
    acc = cb_ref[...][None]
    for k in range(CONV_W):
        acc = acc + cw_ref[k:k + 1, :][None] * xext[:, 5 + k:5 + k + q, :]
    xc = _silu(acc).reshape(rows, CONV_DIM)
    tail = xext[:, q + 5:q + 8, :]
    cnew_ref[...] = tail
    xext[:, 5:8, :] = tail

    xs = xc[:, :SSD_INNER]
    bm = xc[:, SSD_INNER:SSD_INNER + LANES]
    cm = xc[:, SSD_INNER + LANES:CONV_DIM]

    lane = lax.broadcasted_iota(jnp.int32, (1, LANES), 1)
    hmask = lane < SSD_HEADS
    dt = jnp.where(hmask, jax.nn.softplus(dtg_ref[...].reshape(rows, LANES) + dtb_ref[...]), 0.0)
    a = dt * jnp.where(hmask, -jnp.exp(alog_ref[...]), 0.0)

    same, causal = _block_masks(rows, q)
    acum = _dot_sel(causal, a)
    alast = _dot_sel(same, a)

    eexp = eexp_ref[...]
    dt_x = _dot(dt.astype(BF16), eexp)
    e_x = _dot(jnp.exp(acum).astype(BF16), eexp)
    te_x = _dot(jnp.exp(alast - acum).astype(BF16), eexp)
    xdt = xs * dt_x
    xw = xdt * te_x

    cbs = []
    for g in range(SSD_GROUPS):
        cb = _dot_nt(cm[:, g * SSD_STATE:(g + 1) * SSD_STATE].astype(BF16),
                     bm[:, g * SSD_STATE:(g + 1) * SSD_STATE].astype(BF16))
        cbs.append(jnp.where(causal, cb, 0.0))
    ydiag = []
    for h in range(SSD_HEADS):
        col = jnp.broadcast_to(acum[:, h:h + 1], (rows, rows))
        dec = jnp.exp(jnp.where(causal, col - col.T, 0.0))
        w = (cbs[h // ngroup] * dec).astype(BF16)
        ydiag.append(_dot(w, xdt[:, h * SSD_HEADDIM:(h + 1) * SSD_HEADDIM].astype(BF16)))
    y = jnp.concatenate(ydiag, axis=1)

    c_scr[...] = cm
    b_scr[...] = bm
    cdt_scr[...] = jnp.exp(alast).T
    for g in range(SSD_GROUPS):
        xwt_scr[g] = xw[:, g * gw:(g + 1) * gw].T.astype(BF16)

    def seq_step(b, carry):
        r0 = pl.multiple_of(b * q, q)
        rowm, sel = _row_select(rows, SSD_STATE, r0, q)
        for g in range(SSD_GROUPS):
            hg = hst[b, g * ngroup:(g + 1) * ngroup].reshape(gw, SSD_STATE)
            cg = c_scr[pl.ds(r0, q), g * SSD_STATE:(g + 1) * SSD_STATE]
            yoff_scr[pl.ds(r0, q), g * gw:(g + 1) * gw] = _dot_nt(cg.astype(BF16), hg.astype(BF16))
            bmask = jnp.where(rowm, b_scr[:, g * SSD_STATE:(g + 1) * SSD_STATE], 0.0).astype(BF16)
            s_new = _dot(xwt_scr[g], bmask)
            cd = _dot_sel(eexpt_ref[g * gw:(g + 1) * gw, :], _dot_sel_r(cdt_scr[...], sel))
            hst[b, g * ngroup:(g + 1) * ngroup] = (cd * hg + s_new).reshape(ngroup, SSD_HEADDIM, SSD_STATE)
        return carry

    lax.fori_loop(0, bg, seq_step, 0)

    y = y + yoff_scr[...] * e_x + dsk_ref[...] * xs
    yz = y * _silu(z_ref[...].astype(F32).reshape(rows, SSD_INNER))
    ms = jnp.mean(yz * yz, axis=-1, keepdims=True)
    ys_ref[...] = (yz * lax.rsqrt(ms + EPS) * nw_ref[...]).astype(BF16).reshape(bg, q, SSD_INNER)

    @pl.when(c == pl.num_programs(1) - 1)
    def _():
        hout_ref[...] = hst[...]


def _ssd_call(p3, dtg3, conv_prev, h0, conv_w, conv_b, dtb, alog, dsk_x, norm_w, eexp, eexpt, bg, q):
    nb, length, _ = p3.shape
    full = lambda shape: pl.BlockSpec(shape, lambda i, c: (0,) * len(shape))
    rows = bg * q
    gw = SSD_INNER // SSD_GROUPS
    return pl.pallas_call(
        functools.partial(_ssd_body, bg=bg, q=q),
        grid=(nb // bg, length // q),
        in_specs=[
            pl.BlockSpec((bg, q, SSD_INNER), lambda i, c: (i, c, COL_XS // SSD_INNER)),
            pl.BlockSpec((bg, q, 2 * LANES), lambda i, c: (i, c, COL_BC // (2 * LANES))),
            pl.BlockSpec((bg, q, LANES), lambda i, c: (i, c, 0)),
            pl.BlockSpec((bg, q, SSD_INNER), lambda i, c: (i, c, COL_Z // SSD_INNER)),
            pl.BlockSpec((bg, CONV_W - 1, CONV_DIM), lambda i, c: (i, 0, 0)),
            pl.BlockSpec((bg, SSD_HEADS, SSD_HEADDIM, SSD_STATE), lambda i, c: (i, 0, 0, 0)),
            full((CONV_W, CONV_DIM)), full((1, CONV_DIM)), full((1, LANES)), full((1, LANES)),
            full((1, SSD_INNER)), full((1, SSD_INNER)), full((LANES, SSD_INNER)), full((SSD_INNER, LANES)),
        ],
        out_specs=[
            pl.BlockSpec((bg, q, SSD_INNER), lambda i, c: (i, c, 0)),
            pl.BlockSpec((bg, CONV_W - 1, CONV_DIM), lambda i, c: (i, 0, 0)),
            pl.BlockSpec((bg, SSD_HEADS, SSD_HEADDIM, SSD_STATE), lambda i, c: (i, 0, 0, 0)),
        ],
        out_shape=[
            jax.ShapeDtypeStruct((nb, length, SSD_INNER), BF16),
            jax.ShapeDtypeStruct((nb, CONV_W - 1, CONV_DIM), F32),
            jax.ShapeDtypeStruct((nb, SSD_HEADS, SSD_HEADDIM, SSD_STATE), F32),
        ],
        scratch_shapes=[
            pltpu.VMEM((bg, q + 8, CONV_DIM), F32),
            pltpu.VMEM((bg, SSD_HEADS, SSD_HEADDIM, SSD_STATE), F32),
            pltpu.VMEM((rows, LANES), F32),
            pltpu.VMEM((rows, LANES), F32),
            pltpu.VMEM((SSD_GROUPS, gw, rows), BF16),
            pltpu.VMEM((LANES, rows), F32),
            pltpu.VMEM((rows, SSD_INNER), F32),
        ],
        compiler_params=_cparams(("parallel", "arbitrary")),
        name="ssd",
    )(p3, p3, dtg3, p3, conv_prev, h0, conv_w, conv_b, dtb, alog, dsk_x, norm_w, eexp, eexpt)


def _gla_body(q_ref, k_ref, v_ref, r_ref, dtg_ref, s0_ref, w2_ref, b2_ref, nw_ref, o_ref, sout_ref,
              sst, qe_scr, kdt_scr, v_scr, cdt_scr, oint_scr, *, bg, q):
    rows = bg * q
    c = pl.program_id(1)

    @pl.when(c == 0)
    def _():
        sst[...] = s0_ref[...]

    qv = q_ref[...].astype(F32).reshape(rows, GLA_KEY)
    kv = k_ref[...].astype(F32).reshape(rows, GLA_KEY)
    vv = v_ref[...].astype(F32).reshape(rows, GLA_VAL)
    glr = dtg_ref[...].reshape(rows, LANES)
    g = jax.nn.log_sigmoid(_dot(glr.astype(BF16), w2_ref[...]) + b2_ref[...]) * (1.0 / GLA_GATE_NORM)

    same, causal = _block_masks(rows, q)
    gc = _dot_sel(causal, g)
    gl = _dot_sel(same, g)
    qe = qv * (GLA_DK ** -0.5) * jnp.exp(gc)
    ke = kv * jnp.exp(-gc)
    kd = kv * jnp.exp(gl - gc)
    cdk = jnp.exp(gl)

    qe_b = qe.astype(BF16)
    ke_b = ke.astype(BF16)
    v_b = vv.astype(BF16)
    o_intra = []
    for h in range(GLA_HEADS):
        ks = slice(h * GLA_DK, (h + 1) * GLA_DK)
        att = jnp.where(causal, _dot_nt(qe_b[:, ks], ke_b[:, ks]), 0.0)
        o_intra.append(_dot(att.astype(BF16), v_b[:, h * GLA_DV:(h + 1) * GLA_DV]))
        kdt_scr[h] = kd[:, ks].T.astype(BF16)
        cdt_scr[h] = cdk[:, ks].T
    qe_scr[...] = qe
    v_scr[...] = vv

    def seq_step(b, carry):
        r0 = pl.multiple_of(b * q, q)
        rowm, sel = _row_select(rows, LANES, r0, q)
        for h in range(GLA_HEADS):
            s_old = sst[b, h]
            qeb = qe_scr[pl.ds(r0, q), h * GLA_DK:(h + 1) * GLA_DK]
            oint_scr[pl.ds(r0, q), h * GLA_DV:(h + 1) * GLA_DV] = _dot(qeb.astype(BF16), s_old.astype(BF16))
            vm = jnp.where(rowm, v_scr[:, h * GLA_DV:(h + 1) * GLA_DV], 0.0).astype(BF16)
            s_new = _dot(kdt_scr[h], vm)
            cd = _dot_sel_r(cdt_scr[h], sel)
            sst[b, h] = jnp.concatenate([cd] * (GLA_DV // LANES), axis=1) * s_old + s_new
        return carry

    lax.fori_loop(0, bg, seq_step, 0)

    rv = r_ref[...].astype(F32).reshape(rows, GLA_VAL)
    outs = []
    for h in range(GLA_HEADS):
        vs = slice(h * GLA_DV, (h + 1) * GLA_DV)
        o = o_intra[h] + oint_scr[:, vs]
        ms = jnp.mean(o * o, axis=-1, keepdims=True)
        outs.append(o * lax.rsqrt(ms + EPS) * nw_ref[...])
    o_all = jnp.concatenate(outs, axis=1) * _silu(rv)
    o_ref[...] = o_all.astype(BF16).reshape(bg, q, GLA_VAL)

    @pl.when(c == pl.num_programs(1) - 1)
    def _():
        sout_ref[...] = sst[...]


def _gla_call(p3, dtg3, s0, w2_pad, b2, norm_w, bg, q):
    nb, length, _ = p3.shape
    full = lambda shape: pl.BlockSpec(shape, lambda i, c: (0,) * len(shape))
    rows = bg * q
    state_block = (bg, GLA_HEADS, GLA_DK, GLA_DV)
    return pl.pallas_call(
        functools.partial(_gla_body, bg=bg, q=q),
        grid=(nb // bg, length // q),
        in_specs=[
            pl.BlockSpec((bg, q, GLA_KEY), lambda i, c: (i, c, COL_Q // GLA_KEY)),
            pl.BlockSpec((bg, q, GLA_KEY), lambda i, c: (i, c, COL_K // GLA_KEY)),
            pl.BlockSpec((bg, q, GLA_VAL), lambda i, c: (i, c, COL_V // GLA_VAL)),
            pl.BlockSpec((bg, q, GLA_VAL), lambda i, c: (i, c, COL_R // GLA_VAL)),
            pl.BlockSpec((bg, q, LANES), lambda i, c: (i, c, 0)),
            pl.BlockSpec(state_block, lambda i, c: (i, 0, 0, 0)),
            full((LANES, GLA_KEY)), full((1, GLA_KEY)), full((1, GLA_DV)),
        ],
        out_specs=[
            pl.BlockSpec((bg, q, GLA_VAL), lambda i, c: (i, c, 0)),
            pl.BlockSpec(state_block, lambda i, c: (i, 0, 0, 0)),
        ],
        out_shape=[
            jax.ShapeDtypeStruct((nb, length, GLA_VAL), BF16),
            jax.ShapeDtypeStruct((nb, GLA_HEADS, GLA_DK, GLA_DV), F32),
        ],
        scratch_shapes=[
            pltpu.VMEM(state_block, F32),
            pltpu.VMEM((rows, GLA_KEY), F32),
            pltpu.VMEM((GLA_HEADS, GLA_DK, rows), BF16),
            pltpu.VMEM((rows, GLA_VAL), F32),
            pltpu.VMEM((GLA_HEADS, GLA_DK, rows), F32),
            pltpu.VMEM((rows, GLA_VAL), F32),
        ],
        compiler_params=_cparams(("parallel", "arbitrary")),
        name="gla",
    )(p3, p3, p3, p3, dtg3, s0, w2_pad, b2, norm_w)


def _merge_body(x_ref, ys_ref, o_ref, gs_ref, gg_ref, wso_ref, wgo_ref, wo_ref, nw_ref, x1_ref, h2t_ref):
    mix = (jax.nn.sigmoid(gs_ref[...].astype(F32)) * _dot(ys_ref[...], wso_ref[...])
           + jax.nn.sigmoid(gg_ref[...].astype(F32)) * _dot(o_ref[...], wgo_ref[...]))
    x1 = x_ref[...] + _dot(mix.astype(BF16), wo_ref[...])
    x1_ref[...] = x1
    ms = jnp.mean(x1 * x1, axis=-1, keepdims=True)
    h2 = x1 * lax.rsqrt(ms + EPS) * nw_ref[...]
    h2t_ref[...] = h2.T.astype(BF16)


def _merge_call(x2d, ys, o, proj, wso, wgo, wo, norm_w, tm):
    n = x2d.shape[0]
    tok = lambda col: pl.BlockSpec((tm, D_MODEL), lambda i: (i, col))
    wfull = pl.BlockSpec((D_MODEL, D_MODEL), lambda i: (0, 0))
    return pl.pallas_call(
        _merge_body,
        grid=(n // tm,),
        in_specs=[tok(0), tok(0), tok(0), tok(COL_GS // D_MODEL), tok(COL_GG // D_MODEL),
                  wfull, wfull, wfull, pl.BlockSpec((1, D_MODEL), lambda i: (0, 0))],
        out_specs=[pl.BlockSpec((tm, D_MODEL), lambda i: (i, 0)),
                   pl.BlockSpec((D_MODEL, tm), lambda i: (0, i))],
        out_shape=[jax.ShapeDtypeStruct((n, D_MODEL), F32),
                   jax.ShapeDtypeStruct((D_MODEL, n), BF16)],
        compiler_params=_cparams(("parallel",)),
        name="merge",
    )(x2d, ys, o, proj, proj, wso, wgo, wo, norm_w)


NEG_INF = float("-inf")
PAIR_ROWS = ((1, 8), (2, 5), (3, 4), (4, 3), (5, 2), (6, 2), (7, 2))
N_PAIR_PAD = 30


def _count_rows(mask):
    return jnp.sum(jnp.where(mask, 1.0, 0.0), axis=0, keepdims=True)


def _top16_fast(s, want_rank):
    row = lax.broadcasted_iota(jnp.int32, (PEER_TOPK, s.shape[1]), 0)
    vals = jnp.zeros((PEER_TOPK, s.shape[1]), F32)
    rank = jnp.full(s.shape, float(PEER_TOPK), F32) if want_rank else None
    cur = s
    for it in range(PEER_TOPK):
        m = jnp.max(cur, axis=0, keepdims=True)
        vals = jnp.where(row == it, m, vals)
        eq = cur == m
        if want_rank:
            rank = jnp.where(eq, float(it), rank)
        cur = jnp.where(eq, NEG_INF, cur)
    return vals, rank, _count_rows(cur == NEG_INF)


def _top16_exact(s):
    row = lax.broadcasted_iota(jnp.int32, (PEER_TOPK, s.shape[1]), 0)
    idx = lax.broadcasted_iota(jnp.int32, s.shape, 0).astype(F32)
    vals = jnp.zeros((PEER_TOPK, s.shape[1]), F32)
    rank = jnp.full(s.shape, float(PEER_TOPK), F32)
    cur = s
    for it in range(PEER_TOPK):
        m = jnp.max(cur, axis=0, keepdims=True)
        vals = jnp.where(row == it, m, vals)
        hit = idx == jnp.min(jnp.where(cur == m, idx, float(s.shape[0])), axis=0, keepdims=True)
        rank = jnp.where(hit, float(it), rank)
        cur = jnp.where(hit, NEG_INF, cur)
    return vals, rank


def _head_fast(s1, s2, pair_ties=False):
    tt = s1.shape[1]
    row8 = lax.broadcasted_iota(jnp.int32, (SUBLANES, tt), 0)
    v1, _, gone1 = _top16_fast(s1, False)
    v2, rank2, gone2 = _top16_fast(s2, True)
    cands = [v1[0:1] + v2]
    firsts = [jnp.broadcast_to(v1[0:1], (PEER_TOPK, tt))]
    v2lo = v2[0:SUBLANES]
    for a, nb in PAIR_ROWS:
        cands.append(jnp.where(row8 < nb, v1[a:a + 1] + v2lo, NEG_INF))
        firsts.append(jnp.broadcast_to(v1[a:a + 1], (SUBLANES, tt)))
    cands.append(v1[SUBLANES:PEER_TOPK] + v2[0:1])
    firsts.append(v1[SUBLANES:PEER_TOPK])
    cur = jnp.concatenate(cands, axis=0)
    top = v1[0:1] + v2[0:1]
    zsum = jnp.zeros((1, tt), F32)
    tau = top
    cnt16 = jnp.zeros((PEER_TOPK, tt), F32)
    if pair_ties:
        first = jnp.concatenate(firsts, axis=0)
        tau1 = v1[0:1]
        for _ in range(PEER_TOPK):
            tau = jnp.max(cur, axis=0, keepdims=True)
            eq = cur == tau
            tau1 = jnp.max(jnp.where(eq, first, NEG_INF), axis=0, keepdims=True)
            zsum = zsum + jnp.exp(tau - top)
            cur = jnp.where(eq, jnp.where(first == tau1, NEG_INF, cur), cur)
        gone3 = jnp.full((1, tt), float(PEER_TOPK), F32)
        last_ok = jnp.where(v1 >= tau1, 1.0, 0.0)
        for b in range(PEER_TOPK):
            pair = v1 + v2[b:b + 1]
            cnt16 = cnt16 + jnp.where(pair > tau, 1.0, jnp.where(pair == tau, last_ok, 0.0))
    else:
        pairs = cur
        tau_prev = top
        n_last = None
        for it in range(PEER_TOPK):
            tau_prev = tau
            tau = jnp.max(cur, axis=0, keepdims=True)
            eq = cur == tau
            if it == PEER_TOPK - 1:
                n_last = _count_rows(eq)
            cur = jnp.where(eq, NEG_INF, cur)
        gone3 = _count_rows(cur == NEG_INF) - float(N_PAIR_PAD)
        one_extra = jnp.logical_and(gone3 == float(PEER_TOPK + 1), n_last == 1.0)
        tau = jnp.where(one_extra, tau_prev, tau)
        gone3 = jnp.where(one_extra, float(PEER_TOPK), gone3)
        zsum = jnp.sum(jnp.where(pairs >= tau, jnp.exp(pairs - top), 0.0), axis=0, keepdims=True)
        for b in range(PEER_TOPK):
            cnt16 = cnt16 + jnp.where(v1 + v2[b:b + 1] >= tau, 1.0, 0.0)
    cnt = jnp.zeros((N_KEYS, tt), F32)
    for a in range(PEER_TOPK):
        cnt = jnp.where(s1 == v1[a:a + 1], cnt16[a:a + 1], cnt)
    tables = (rank2, jnp.exp(s2 - v2[0:1]), cnt, jnp.exp(s1 - v1[0:1]) / zsum)
    return tables, jnp.maximum(gone1, gone2), gone3


def _head_exact(s1, s2):
    tt = s1.shape[1]
    v1, rank1 = _top16_exact(s1)
    v2, rank2 = _top16_exact(s2)
    sums = jnp.concatenate([v1[a:a + 1] + v2 for a in range(PEER_TOPK)], axis=0)
    n_pairs = PEER_TOPK * PEER_TOPK
    flat = lax.broadcasted_iota(jnp.int32, (n_pairs, tt), 0).astype(F32)
    top = v1[0:1] + v2[0:1]
    zsum = jnp.zeros((1, tt), F32)
    cur = sums
    tau = top
    last = jnp.zeros((1, tt), F32)
    for _ in range(PEER_TOPK):
        tau = jnp.max(cur, axis=0, keepdims=True)
        last = jnp.min(jnp.where(cur == tau, flat, float(n_pairs)), axis=0, keepdims=True)
        zsum = zsum + jnp.exp(tau - top)
        cur = jnp.where(flat == last, NEG_INF, cur)
    taken = jnp.where(sums > tau, 1.0, jnp.where(sums == tau, jnp.where(flat <= last, 1.0, 0.0), 0.0))
    cnt = jnp.zeros((N_KEYS, tt), F32)
    for a in range(PEER_TOPK):
        cnt_a = jnp.sum(taken[a * PEER_TOPK:(a + 1) * PEER_TOPK], axis=0, keepdims=True)
        cnt = jnp.where(rank1 == float(a), cnt_a, cnt)
    return rank2, jnp.exp(s2 - v2[0:1]), cnt, jnp.exp(s1 - v1[0:1]) / zsum


def _topk_body(h2t_ref, wqt_ref, keys_ref, rk2_ref, e2_ref, cnt_ref, c_ref, s_scr, tie_scr):
    qt = _dot(wqt_ref[...], h2t_ref[...])

    def store(h, tables):
        rank2, e2, cnt, c = tables
        rk2_ref[h] = rank2.astype(BF16)
        e2_ref[h] = e2.astype(BF16)
        cnt_ref[h] = cnt
        c_ref[h] = c

    for h in range(PEER_HEADS):
        base = h * 2 * PEER_HALF
        s1 = _dot(keys_ref[h, 0], qt[base:base + PEER_HALF].astype(BF16))
        s2 = _dot(keys_ref[h, 1], qt[base + PEER_HALF:base + 2 * PEER_HALF].astype(BF16))
        tables, gone_half, gone_pair = _head_fast(s1, s2)
        store(h, tables)
        s_scr[2 * h] = s1
        s_scr[2 * h + 1] = s2
        half_tie = (jnp.max(gone_half) > float(PEER_TOPK)).astype(jnp.int32)
        pair_tie = (jnp.max(gone_pair) > float(PEER_TOPK)).astype(jnp.int32)
        tie_scr[h] = jnp.maximum(2 * half_tie, pair_tie)

    def redo(h, carry):
        @pl.when(tie_scr[h] == 2)
        def _():
            store(h, _head_exact(s_scr[2 * h], s_scr[2 * h + 1]))

        @pl.when(tie_scr[h] == 1)
        def _():
            store(h, _head_fast(s_scr[2 * h], s_scr[2 * h + 1], pair_ties=True)[0])
        return carry

    lax.fori_loop(0, PEER_HEADS, redo, 0)


def _topk_call(h2t, wqt, keys, tt):
    n = h2t.shape[1]
    tab = pl.BlockSpec((PEER_HEADS, N_KEYS, tt), lambda i: (0, 0, i))
    tab_f32 = jax.ShapeDtypeStruct((PEER_HEADS, N_KEYS, n), F32)
    tab_b16 = jax.ShapeDtypeStruct((PEER_HEADS, N_KEYS, n), BF16)
    return pl.pallas_call(
        _topk_body,
        grid=(n // tt,),
        in_specs=[pl.BlockSpec((D_MODEL, tt), lambda i: (0, i)),
                  pl.BlockSpec(wqt.shape, lambda i: (0, 0)),
                  pl.BlockSpec(keys.shape, lambda i: (0, 0, 0, 0))],
        out_specs=[tab, tab, tab, tab],
        out_shape=[tab_b16, tab_b16, tab_f32, tab_f32],
        scratch_shapes=[pltpu.VMEM((2 * PEER_HEADS, N_KEYS, tt), F32), pltpu.SMEM((PEER_HEADS,), jnp.int32)],
        compiler_params=_cparams(("parallel",)),
        name="topk",
    )(h2t, wqt, keys)


def _gelu(x):
    x = x.astype(BF16)
    return (0.5 * x) * (1.0 + lax.erf(x * (2.0 ** -0.5)))


def _peer_gate_mul(act_scr, w_scr, rk2_ref, e2_ref, cnt_ref, c_ref, chunk, groups):
    tt = act_scr.shape[1]
    zero = jnp.zeros((), BF16)
    key0 = pl.multiple_of(chunk * groups, groups)
    for ii in range(groups):
        gate = None
        for h in range(PEER_HEADS):
            cnt_row = cnt_ref[h, pl.ds(key0, groups), :][ii:ii + 1]
            c_row = c_ref[h, pl.ds(key0, groups), :][ii:ii + 1]
            cnt_t = jnp.broadcast_to(cnt_row, (N_KEYS, tt)).astype(BF16)
            c_t = jnp.broadcast_to(c_row, (N_KEYS, tt)).astype(BF16)
            term = jnp.where(rk2_ref[h] < cnt_t, e2_ref[h], zero) * c_t
            gate = term if gate is None else gate + term
        rows = slice(ii * N_KEYS, (ii + 1) * N_KEYS)
        w_scr[rows, :] = act_scr[rows, :] * gate


def _peer_body(h2t_ref, x1_ref, rk2_ref, e2_ref, cnt_ref, c_ref, u_ref, vt_ref, nw_ref, y_ref,
               acc_scr, act_scr, w_scr, *, ec):
    e = pl.program_id(1)
    groups = ec // N_KEYS

    @pl.when(e == 0)
    def _():
        acc_scr[...] = jnp.zeros_like(acc_scr)

    act_scr[...] = _gelu(_dot(u_ref[...], h2t_ref[...]))
    _peer_gate_mul(act_scr, w_scr, rk2_ref, e2_ref, cnt_ref, c_ref, e, groups)
    acc_scr[...] += _dot(vt_ref[...], w_scr[...])

    @pl.when(e == pl.num_programs(1) - 1)
    def _():
        x2 = x1_ref[...] + acc_scr[...].T
        ms = jnp.mean(x2 * x2, axis=-1, keepdims=True)
        y_ref[...] = x2 * lax.rsqrt(ms + EPS) * nw_ref[...]


def _peer_call(h2t, x1, rk2, e2, cnt, cc, u_b, vt_b, norm_w, tt, ec):
    n = x1.shape[0]
    assert (ec // N_KEYS) % SUBLANES == 0, "an expert chunk covers whole sublane tiles of first-half keys"
    tab = pl.BlockSpec((PEER_HEADS, N_KEYS, tt), lambda i, e: (0, 0, i))
    chunk_shape = pltpu.VMEM((ec, tt), BF16)
    return pl.pallas_call(
        functools.partial(_peer_body, ec=ec),
        grid=(n // tt, u_b.shape[0] // ec),
        in_specs=[pl.BlockSpec((D_MODEL, tt), lambda i, e: (0, i)),
                  pl.BlockSpec((tt, D_MODEL), lambda i, e: (i, 0)),
                  tab, tab, tab, tab,
                  pl.BlockSpec((ec, D_MODEL), lambda i, e: (e, 0)),
                  pl.BlockSpec((D_MODEL, ec), lambda i, e: (0, e)),
                  pl.BlockSpec((1, D_MODEL), lambda i, e: (0, 0))],
        out_specs=pl.BlockSpec((tt, D_MODEL), lambda i, e: (i, 0)),
        out_shape=jax.ShapeDtypeStruct((n, D_MODEL), F32),
        scratch_shapes=[pltpu.VMEM((D_MODEL, tt), F32), chunk_shape, chunk_shape],
        compiler_params=_cparams(("parallel", "arbitrary")),
        name="peer",
    )(h2t, x1, rk2, e2, cnt, cc, u_b, vt_b, norm_w)


def _permute_w_in(w):
    splits = [sum(IN_SIZES[:i + 1]) for i in range(len(IN_SIZES) - 1)]
    z, xbc, dt, q, k, v, glr, r, gs, gg = jnp.split(w, splits, axis=1)
    pad = jnp.zeros((D_MODEL, PROJ_COLS - COL_DTG - SSD_HEADS - GLA_GATE_RANK), w.dtype)
    return jnp.concatenate([z, v, r, gs, gg, q, k, xbc, dt, glr, pad], axis=1).astype(BF16)


def _row(v, width=None):
    v = v.reshape(1, -1).astype(F32)
    if width is not None and v.shape[1] < width:
        v = jnp.pad(v, ((0, 0), (0, width - v.shape[1])))
    return v


def _trunk(x, conv0, ssd0, gla0, wts, bg_ssd, q_ssd, bg_gla, q_gla, tm, tt, ec, proj_dtype):
    nb, length, _ = x.shape
    n = nb * length
    x2d = x.reshape(n, D_MODEL)
    proj, dtg = _proj_call(x2d, wts["norm_mix"], wts["w_in"], tm, PROJ_COLS // 3, proj_dtype)
    p3 = proj.reshape(nb, length, PROJ_COLS)
    dtg3 = dtg.reshape(nb, length, LANES)
    ys, conv_new, ssd_new = _ssd_call(p3, dtg3, conv0, ssd0, wts["conv_w"], wts["conv_b"], wts["dtb"], wts["alog"],
                                      wts["dsk"], wts["ssd_norm"], wts["eexp"], wts["eexpt"], bg_ssd, q_ssd)
    o, gla_new = _gla_call(p3, dtg3, gla0, wts["w2"], wts["b2"], wts["gla_norm"], bg_gla, q_gla)
    x1, h2t = _merge_call(x2d, ys.reshape(n, SSD_INNER), o.reshape(n, GLA_VAL), proj,
                          wts["wso"], wts["wgo"], wts["wo"], wts["norm_ffn"], tm // 2)
    rk2, e2, cnt, cc = _topk_call(h2t, wts["wqt"], wts["keys"], tt)
    y = _peer_call(h2t, x1, rk2, e2, cnt, cc, wts["u"], wts["vt"], wts["final_norm"], tt, ec)
    return y.reshape(nb, length, D_MODEL), conv_new[None], ssd_new[None], gla_new[None]


def _weights(p):
    assert p["norm_mix_w"].shape[0] == 1, "one layer"
    w2_pad = jnp.zeros((LANES, GLA_KEY), F32).at[SSD_HEADS:SSD_HEADS + GLA_GATE_RANK].set(p["w_gla_gate2"][0])
    head_of_lane = jnp.arange(SSD_INNER) // SSD_HEADDIM
    return {
        "norm_mix": _row(p["norm_mix_w"][0]),
        "w_in": _permute_w_in(p["w_in"][0]),
        "conv_w": p["conv_w"][0], "conv_b": _row(p["conv_b"][0]),
        "dtb": _row(p["dt_bias"][0], LANES), "alog": _row(p["a_log"][0], LANES),
        "dsk": _row(jnp.repeat(p["d_skip"][0], SSD_HEADDIM)),
        "ssd_norm": _row(p["ssd_norm_w"][0]),
        "eexp": (jnp.arange(LANES)[:, None] == head_of_lane[None, :]).astype(BF16),
        "eexpt": (head_of_lane[:, None] == jnp.arange(LANES)[None, :]).astype(BF16),
        "w2": w2_pad.astype(BF16), "b2": _row(p["b_gla_gate"][0]), "gla_norm": _row(p["gla_norm_w"][0]),
        "wso": p["w_ssd_out"][0].astype(BF16), "wgo": p["w_gla_out"][0].astype(BF16),
        "wo": p["w_out"][0].astype(BF16),
        "norm_ffn": _row(p["norm_ffn_w"][0]),
        "wqt": p["w_query"][0].T.astype(BF16), "keys": p["sub_keys"][0].astype(BF16),
        "u": p["expert_u"][0].astype(BF16), "vt": p["expert_v"][0].T.astype(BF16),
        "final_norm": _row(p["final_norm_w"]),
    }


def kernel(x_prompt, x_sample, state_conv, state_ssd, state_gla, norm_mix_w, w_in, conv_w, conv_b, dt_bias,
           a_log, d_skip, ssd_norm_w, w_gla_gate2, b_gla_gate, gla_norm_w, w_ssd_out, w_gla_out, w_out,
           norm_ffn_w, w_query, sub_keys, expert_u, expert_v, final_norm_w):
    wts = _weights(dict(
        norm_mix_w=norm_mix_w, w_in=w_in, conv_w=conv_w, conv_b=conv_b, dt_bias=dt_bias, a_log=a_log,
        d_skip=d_skip, ssd_norm_w=ssd_norm_w, w_gla_gate2=w_gla_gate2, b_gla_gate=b_gla_gate,
        gla_norm_w=gla_norm_w, w_ssd_out=w_ssd_out, w_gla_out=w_gla_out, w_out=w_out, norm_ffn_w=norm_ffn_w,
        w_query=w_query, sub_keys=sub_keys, expert_u=expert_u, expert_v=expert_v, final_norm_w=final_norm_w))
    bp = x_prompt.shape[0]
    conv0 = jnp.zeros((bp, CONV_W - 1, CONV_DIM), F32)
    ssd0 = jnp.zeros((bp, SSD_HEADS, SSD_HEADDIM, SSD_STATE), F32)
    gla0 = jnp.zeros((bp, GLA_HEADS, GLA_DK, GLA_DV), F32)
    ls = x_sample.shape[1]
    y_p, conv_p, ssd_p, gla_p = _trunk(x_prompt, conv0, ssd0, gla0, wts,
                                       bg_ssd=1, q_ssd=ROWS, bg_gla=2, q_gla=ROWS // 2, tm=1024, tt=512, ec=2048,
                                       proj_dtype=BF16)
    y_s, conv_s, ssd_s, gla_s = _trunk(x_sample, state_conv[0], state_ssd[0], state_gla[0], wts,
                                       bg_ssd=ROWS // ls, q_ssd=ls, bg_gla=ROWS // ls, q_gla=ls,
                                       tm=1024, tt=512, ec=2048, proj_dtype=F32)
    return (y_p, y_s, conv_p, ssd_p, gla_p, conv_s, ssd_s, gla_s)
```

```python
import functools
import math

import jax
import jax.numpy as jnp
from jax import lax
from jax.experimental import pallas as pl
from jax.experimental.pallas import tpu as pltpu

F32 = jnp.float32
BF16 = jnp.bfloat16

D_MODEL = 1024
SSD_HEADS = 16
SSD_HEADDIM = 64
SSD_INNER = 1024
SSD_GROUPS = 2
SSD_STATE = 64
CONV_W = 4
CONV_DIM = 1280
GLA_HEADS = 4
GLA_DK = 128
GLA_DV = 256
GLA_KEY = 512
GLA_VAL = 1024
GLA_GATE_RANK = 16
GLA_GATE_NORM = 16.0
IN_SIZES = (SSD_INNER, CONV_DIM, SSD_HEADS, GLA_KEY, GLA_KEY, GLA_VAL, GLA_GATE_RANK, GLA_VAL, D_MODEL, D_MODEL)
PEER_HEADS = 8
N_KEYS = 128
PEER_HALF = 128
PEER_TOPK = 16
EPS = 1e-6

LANES = 128
SUBLANES = 8
ROWS = 128
VMEM_LIMIT = 56 * 1024 * 1024

COL_Z, COL_V, COL_R, COL_GS, COL_GG = 0, 1024, 2048, 3072, 4096
COL_Q, COL_K, COL_XS, COL_BC, COL_DTG = 5120, 5632, 6144, 7168, 7424
PROJ_COLS = 7680

NT_DIMS = (((1,), (1,)), ((), ()))


def _cparams(sem):
    return pltpu.CompilerParams(dimension_semantics=sem, vmem_limit_bytes=VMEM_LIMIT)


def _dot(a, b):
    return jnp.dot(a, b, preferred_element_type=F32)


def _split3(x):
    hi = x.astype(BF16)
    r = x - hi.astype(F32)
    mid = r.astype(BF16)
    return hi, mid, (r - mid.astype(F32)).astype(BF16)


def _dot_sel(sel, x):
    hi, mid, lo = _split3(x)
    sel = sel.astype(BF16)
    return _dot(sel, hi) + (_dot(sel, mid) + _dot(sel, lo))


def _dot_sel_r(x, sel):
    hi, mid, lo = _split3(x)
    sel = sel.astype(BF16)
    return _dot(hi, sel) + (_dot(mid, sel) + _dot(lo, sel))


def _dot_nt(a, b):
    return lax.dot_general(a, b, NT_DIMS, preferred_element_type=F32)


def _silu(x):
    return x * jax.nn.sigmoid(x)


def _proj_body(x_ref, nw_ref, w_ref, o_ref, dtg_ref, h_scr, *, dtg_tile, dtg_off):
    @pl.when(pl.program_id(1) == 0)
    def _():
        x = x_ref[...]
        ms = jnp.mean(x * x, axis=-1, keepdims=True)
        h_scr[...] = (x * lax.rsqrt(ms + EPS) * nw_ref[...]).astype(BF16)

    acc = _dot(h_scr[...], w_ref[...])
    o_ref[...] = acc.astype(o_ref.dtype)

    @pl.when(pl.program_id(1) == dtg_tile)
    def _():
        dtg_ref[...] = acc[:, dtg_off:dtg_off + LANES]


def _proj_call(x2d, norm_w, w_perm, tm, tn, out_dtype):
    n = x2d.shape[0]
    dtg_tile, dtg_off = divmod(COL_DTG, tn)
    assert dtg_off + LANES <= tn
    return pl.pallas_call(
        functools.partial(_proj_body, dtg_tile=dtg_tile, dtg_off=dtg_off),
        grid=(n // tm, PROJ_COLS // tn),
        in_specs=[
            pl.BlockSpec((tm, D_MODEL), lambda i, j: (i, 0)),
            pl.BlockSpec((1, D_MODEL), lambda i, j: (0, 0)),
            pl.BlockSpec((D_MODEL, tn), lambda i, j: (0, j)),
        ],
        out_specs=[pl.BlockSpec((tm, tn), lambda i, j: (i, j)),
                   pl.BlockSpec((tm, LANES), lambda i, j: (i, 0))],
        out_shape=[jax.ShapeDtypeStruct((n, PROJ_COLS), out_dtype),
                   jax.ShapeDtypeStruct((n, LANES), F32)],
        scratch_shapes=[pltpu.VMEM((tm, D_MODEL), BF16)],
        compiler_params=_cparams(("parallel", "arbitrary")),
        name="proj",
    )(x2d, norm_w, w_perm)


def _block_masks(rows, q):
    ri = lax.broadcasted_iota(jnp.int32, (rows, rows), 0)
    ci = lax.broadcasted_iota(jnp.int32, (rows, rows), 1)
    shift = int(math.log2(q))
    same = (ri >> shift) == (ci >> shift)
    causal = same & (ci <= ri)
    return same, causal


def _row_select(rows, width, r0, q):
    rcol = lax.broadcasted_iota(jnp.int32, (rows, 1), 0)
    rowm = (rcol >= r0) & (rcol < r0 + q)
    rfull = lax.broadcasted_iota(jnp.int32, (rows, width), 0)
    sel = jnp.where(rfull == r0 + (q - 1), 1.0, 0.0).astype(F32)
    return rowm, sel


def _ssd_body(xs_ref, bc_ref, dtg_ref, z_ref, cprev_ref, h0_ref, cw_ref, cb_ref, dtb_ref, alog_ref,
              dsk_ref, nw_ref, eexp_ref, eexpt_ref, ys_ref, cnew_ref, hout_ref,
              xext, hst, c_scr, b_scr, xwt_scr, cdt_scr, yoff_scr, *, bg, q):
    rows = bg * q
    c = pl.program_id(1)
    ngroup = SSD_HEADS // SSD_GROUPS
    gw = ngroup * SSD_HEADDIM

    @pl.when(c == 0)
    def _():
        xext[:, 5:8, :] = cprev_ref[...]
        hst[...] = h0_ref[...]

    xext[:, 8:8 + q, 0:SSD_INNER] = xs_ref[...].astype(F32)
    xext[:, 8:8 + q, SSD_INNER:CONV_DIM] = bc_ref[...].astype(F32)
    acc = cb_ref[...][None]
    for k in range(CONV_W):
        acc = acc + cw_ref[k:k + 1, :][None] * xext[:, 5 + k:5 + k + q, :]
    xc = _silu(acc).reshape(rows, CONV_DIM)
    tail = xext[:, q + 5:q + 8, :]
    cnew_ref[...] = tail
    xext[:, 5:8, :] = tail

    xs = xc[:, :SSD_INNER]
    bm = xc[:, SSD_INNER:SSD_INNER + LANES]
    cm = xc[:, SSD_INNER + LANES:CONV_DIM]

    lane = lax.broadcasted_iota(jnp.int32, (1, LANES), 1)
    hmask = lane < SSD_HEADS
    dt = jnp.where(hmask, jax.nn.softplus(dtg_ref[...].reshape(rows, LANES) + dtb_ref[...]), 0.0)
    a = dt * jnp.where(hmask, -jnp.exp(alog_ref[...]), 0.0)

    same, causal = _block_masks(rows, q)
    acum = _dot_sel(causal, a)
    alast = _dot_sel(same, a)

    eexp = eexp_ref[...]
    dt_x = _dot(dt.astype(BF16), eexp)
    e_x = _dot(jnp.exp(acum).astype(BF16), eexp)
    te_x = _dot(jnp.exp(alast - acum).astype(BF16), eexp)
    xdt = xs * dt_x
    xw = xdt * te_x

    cbs = []
    for g in range(SSD_GROUPS):
        cb = _dot_nt(cm[:, g * SSD_STATE:(g + 1) * SSD_STATE].astype(BF16),
                     bm[:, g * SSD_STATE:(g + 1) * SSD_STATE].astype(BF16))
        cbs.append(jnp.where(causal, cb, 0.0))
    ydiag = []
    for h in range(SSD_HEADS):
        col = jnp.broadcast_to(acum[:, h:h + 1], (rows, rows))
        dec = jnp.exp(jnp.where(causal, col - col.T, 0.0))
        w = (cbs[h // ngroup] * dec).astype(BF16)
        ydiag.append(_dot(w, xdt[:, h * SSD_HEADDIM:(h + 1) * SSD_HEADDIM].astype(BF16)))
    y = jnp.concatenate(ydiag, axis=1)

    c_scr[...] = cm
    b_scr[...] = bm
    cdt_scr[...] = jnp.exp(alast).T
    for g in range(SSD_GROUPS):
        xwt_scr[g] = xw[:, g * gw:(g + 1) * gw].T.astype(BF16)

    def seq_step(b, carry):
        r0 = b * q if isinstance(b, int) else pl.multiple_of(b * q, q)
        rowm, sel = _row_select(rows, SSD_STATE, r0, q)
        for g in range(SSD_GROUPS):
            hg = hst[b, g * ngroup:(g + 1) * ngroup].reshape(gw, SSD_STATE)
            cg = c_scr[pl.ds(r0, q), g * SSD_STATE:(g + 1) * SSD_STATE]
            yoff_scr[pl.ds(r0, q), g * gw:(g + 1) * gw] = _dot_nt(cg.astype(BF16), hg.astype(BF16))
            bmask = jnp.where(rowm, b_scr[:, g * SSD_STATE:(g + 1) * SSD_STATE], 0.0).astype(BF16)
            s_new = _dot(xwt_scr[g], bmask)
            cd = _dot_sel(eexpt_ref[g * gw:(g + 1) * gw, :], _dot_sel_r(cdt_scr[...], sel))
            hst[b, g * ngroup:(g + 1) * ngroup] = (cd * hg + s_new).reshape(ngroup, SSD_HEADDIM, SSD_STATE)
        return carry

    if bg <= 2:
        for b in range(bg):
            seq_step(b, 0)
    else:
        lax.fori_loop(0, bg, seq_step, 0)

    y = y + yoff_scr[...] * e_x + dsk_ref[...] * xs
    yz = y * _silu(z_ref[...].astype(F32).reshape(rows, SSD_INNER))
    ms = jnp.mean(yz * yz, axis=-1, keepdims=True)
    ys_ref[...] = (yz * lax.rsqrt(ms + EPS) * nw_ref[...]).astype(BF16).reshape(bg, q, SSD_INNER)

    @pl.when(c == pl.num_programs(1) - 1)
    def _():
        hout_ref[...] = hst[...]


def _ssd_call(p3, dtg3, conv_prev, h0, conv_w, conv_b, dtb, alog, dsk_x, norm_w, eexp, eexpt, bg, q):
    nb, length, _ = p3.shape
    full = lambda shape: pl.BlockSpec(shape, lambda i, c: (0,) * len(shape))
    rows = bg * q
    gw = SSD_INNER // SSD_GROUPS
    return pl.pallas_call(
        functools.partial(_ssd_body, bg=bg, q=q),
        grid=(nb // bg, length // q),
        in_specs=[
            pl.BlockSpec((bg, q, SSD_INNER), lambda i, c: (i, c, COL_XS // SSD_INNER)),
            pl.BlockSpec((bg, q, 2 * LANES), lambda i, c: (i, c, COL_BC // (2 * LANES))),
            pl.BlockSpec((bg, q, LANES), lambda i, c: (i, c, 0)),
            pl.BlockSpec((bg, q, SSD_INNER), lambda i, c: (i, c, COL_Z // SSD_INNER)),
            pl.BlockSpec((bg, CONV_W - 1, CONV_DIM), lambda i, c: (i, 0, 0)),
            pl.BlockSpec((bg, SSD_HEADS, SSD_HEADDIM, SSD_STATE), lambda i, c: (i, 0, 0, 0)),
            full((CONV_W, CONV_DIM)), full((1, CONV_DIM)), full((1, LANES)), full((1, LANES)),
            full((1, SSD_INNER)), full((1, SSD_INNER)), full((LANES, SSD_INNER)), full((SSD_INNER, LANES)),
        ],
        out_specs=[
            pl.BlockSpec((bg, q, SSD_INNER), lambda i, c: (i, c, 0)),
            pl.BlockSpec((bg, CONV_W - 1, CONV_DIM), lambda i, c: (i, 0, 0)),
            pl.BlockSpec((bg, SSD_HEADS, SSD_HEADDIM, SSD_STATE), lambda i, c: (i, 0, 0, 0)),
        ],
        out_shape=[
            jax.ShapeDtypeStruct((nb, length, SSD_INNER), BF16),
            jax.ShapeDtypeStruct((nb, CONV_W - 1, CONV_DIM), F32),
            jax.ShapeDtypeStruct((nb, SSD_HEADS, SSD_HEADDIM, SSD_STATE), F32),
        ],
        scratch_shapes=[
            pltpu.VMEM((bg, q + 8, CONV_DIM), F32),
            pltpu.VMEM((bg, SSD_HEADS, SSD_HEADDIM, SSD_STATE), F32),
            pltpu.VMEM((rows, LANES), F32),
            pltpu.VMEM((rows, LANES), F32),
            pltpu.VMEM((SSD_GROUPS, gw, rows), BF16),
            pltpu.VMEM((LANES, rows), F32),
            pltpu.VMEM((rows, SSD_INNER), F32),
        ],
        compiler_params=_cparams(("parallel", "arbitrary")),
        name="ssd",
    )(p3, p3, dtg3, p3, conv_prev, h0, conv_w, conv_b, dtb, alog, dsk_x, norm_w, eexp, eexpt)


def _gla_body(q_ref, k_ref, v_ref, r_ref, dtg_ref, s0_ref, w2_ref, b2_ref, nw_ref, o_ref, sout_ref,
              sst, qe_scr, kdt_scr, v_scr, cdt_scr, oint_scr, *, bg, q):
    rows = bg * q
    c = pl.program_id(1)

    @pl.when(c == 0)
    def _():
        sst[...] = s0_ref[...]

    qv = q_ref[...].astype(F32).reshape(rows, GLA_KEY)
    kv = k_ref[...].astype(F32).reshape(rows, GLA_KEY)
    vv = v_ref[...].astype(F32).reshape(rows, GLA_VAL)
    glr = dtg_ref[...].reshape(rows, LANES)
    g = jax.nn.log_sigmoid(_dot(glr.astype(BF16), w2_ref[...]) + b2_ref[...]) * (1.0 / GLA_GATE_NORM)

    same, causal = _block_masks(rows, q)
    gc = _dot_sel(causal, g)
    gl = _dot_sel(same, g)
    qe = qv * (GLA_DK ** -0.5) * jnp.exp(gc)
    ke = kv * jnp.exp(-gc)
    kd = kv * jnp.exp(gl - gc)
    cdk = jnp.exp(gl)

    qe_b = qe.astype(BF16)
    ke_b = ke.astype(BF16)
    v_b = vv.astype(BF16)
    o_intra = []
    for h in range(GLA_HEADS):
        ks = slice(h * GLA_DK, (h + 1) * GLA_DK)
        att = jnp.where(causal, _dot_nt(qe_b[:, ks], ke_b[:, ks]), 0.0)
        o_intra.append(_dot(att.astype(BF16), v_b[:, h * GLA_DV:(h + 1) * GLA_DV]))
        kdt_scr[h] = kd[:, ks].T.astype(BF16)
        cdt_scr[h] = cdk[:, ks].T
    qe_scr[...] = qe
    v_scr[...] = vv

    def seq_step(b, carry):
        r0 = b * q if isinstance(b, int) else pl.multiple_of(b * q, q)
        rowm, sel = _row_select(rows, LANES, r0, q)
        for h in range(GLA_HEADS):
            s_old = sst[b, h]
            qeb = qe_scr[pl.ds(r0, q), h * GLA_DK:(h + 1) * GLA_DK]
            oint_scr[pl.ds(r0, q), h * GLA_DV:(h + 1) * GLA_DV] = _dot(qeb.astype(BF16), s_old.astype(BF16))
            vm = jnp.where(rowm, v_scr[:, h * GLA_DV:(h + 1) * GLA_DV], 0.0).astype(BF16)
            s_new = _dot(kdt_scr[h], vm)
            cd = _dot_sel_r(cdt_scr[h], sel)
            sst[b, h] = jnp.concatenate([cd] * (GLA_DV // LANES), axis=1) * s_old + s_new
        return carry

    if bg <= 2:
        for b in range(bg):
            seq_step(b, 0)
    else:
        lax.fori_loop(0, bg, seq_step, 0)

    rv = r_ref[...].astype(F32).reshape(rows, GLA_VAL)
    outs = []
    for h in range(GLA_HEADS):
        vs = slice(h * GLA_DV, (h + 1) * GLA_DV)
        o = o_intra[h] + oint_scr[:, vs]
        ms = jnp.mean(o * o, axis=-1, keepdims=True)
        outs.append(o * lax.rsqrt(ms + EPS) * nw_ref[...])
    o_all = jnp.concatenate(outs, axis=1) * _silu(rv)
    o_ref[...] = o_all.astype(BF16).reshape(bg, q, GLA_VAL)

    @pl.when(c == pl.num_programs(1) - 1)
    def _():
        sout_ref[...] = sst[...]


def _gla_call(p3, dtg3, s0, w2_pad, b2, norm_w, bg, q):
    nb, length, _ = p3.shape
    full = lambda shape: pl.BlockSpec(shape, lambda i, c: (0,) * len(shape))
    rows = bg * q
    state_block = (bg, GLA_HEADS, GLA_DK, GLA_DV)
    return pl.pallas_call(
        functools.partial(_gla_body, bg=bg, q=q),
        grid=(nb // bg, length // q),
        in_specs=[
            pl.BlockSpec((bg, q, GLA_KEY), lambda i, c: (i, c, COL_Q // GLA_KEY)),
            pl.BlockSpec((bg, q, GLA_KEY), lambda i, c: (i, c, COL_K // GLA_KEY)),
            pl.BlockSpec((bg, q, GLA_VAL), lambda i, c: (i, c, COL_V // GLA_VAL)),
            pl.BlockSpec((bg, q, GLA_VAL), lambda i, c: (i, c, COL_R // GLA_VAL)),
            pl.BlockSpec((bg, q, LANES), lambda i, c: (i, c, 0)),
            pl.BlockSpec(state_block, lambda i, c: (i, 0, 0, 0)),
            full((LANES, GLA_KEY)), full((1, GLA_KEY)), full((1, GLA_DV)),
        ],
        out_specs=[
            pl.BlockSpec((bg, q, GLA_VAL), lambda i, c: (i, c, 0)),
            pl.BlockSpec(state_block, lambda i, c: (i, 0, 0, 0)),
        ],
        out_shape=[
            jax.ShapeDtypeStruct((nb, length, GLA_VAL), BF16),
            jax.ShapeDtypeStruct((nb, GLA_HEADS, GLA_DK, GLA_DV), F32),
        ],
        scratch_shapes=[
            pltpu.VMEM(state_block, F32),
            pltpu.VMEM((rows, GLA_KEY), F32),
            pltpu.VMEM((GLA_HEADS, GLA_DK, rows), BF16),
            pltpu.VMEM((rows, GLA_VAL), F32),
            pltpu.VMEM((GLA_HEADS, GLA_DK, rows), F32),
            pltpu.VMEM((rows, GLA_VAL), F32),
        ],
        compiler_params=_cparams(("parallel", "arbitrary")),
        name="gla",
    )(p3, p3, p3, p3, dtg3, s0, w2_pad, b2, norm_w)


def _merge_body(x_ref, ys_ref, o_ref, gs_ref, gg_ref, wso_ref, wgo_ref, wo_ref, nw_ref, x1_ref, h2t_ref):
    mix = (jax.nn.sigmoid(gs_ref[...].astype(F32)) * _dot(ys_ref[...], wso_ref[...])
           + jax.nn.sigmoid(gg_ref[...].astype(F32)) * _dot(o_ref[...], wgo_ref[...]))
    x1 = x_ref[...] + _dot(mix.astype(BF16), wo_ref[...])
    x1_ref[...] = x1
    ms = jnp.mean(x1 * x1, axis=-1, keepdims=True)
    h2 = x1 * lax.rsqrt(ms + EPS) * nw_ref[...]
    h2t_ref[...] = h2.T.astype(BF16)


def _merge_call(x2d, ys, o, proj, wso, wgo, wo, norm_w, tm):
    n = x2d.shape[0]
    tok = lambda col: pl.BlockSpec((tm, D_MODEL), lambda i: (i, col))
    wfull = pl.BlockSpec((D_MODEL, D_MODEL), lambda i: (0, 0))
    return pl.pallas_call(
        _merge_body,
        grid=(n // tm,),
        in_specs=[tok(0), tok(0), tok(0), tok(COL_GS // D_MODEL), tok(COL_GG // D_MODEL),
                  wfull, wfull, wfull, pl.BlockSpec((1, D_MODEL), lambda i: (0, 0))],
        out_specs=[pl.BlockSpec((tm, D_MODEL), lambda i: (i, 0)),
                   pl.BlockSpec((D_MODEL, tm), lambda i: (0, i))],
        out_shape=[jax.ShapeDtypeStruct((n, D_MODEL), F32),
                   jax.ShapeDtypeStruct((D_MODEL, n), BF16)],
        compiler_params=_cparams(("parallel",)),
        name="merge",
    )(x2d, ys, o, proj, proj, wso, wgo, wo, norm_w)


NEG_INF = float("-inf")
PAIR_ROWS = ((1, 8), (2, 5), (3, 4), (4, 3), (5, 2), (6, 2), (7, 2))
N_PAIR_PAD = 30


def _count_rows(mask):
    return jnp.sum(jnp.where(mask, 1.0, 0.0), axis=0, keepdims=True)


def _top16_fast(s, want_rank):
    row = lax.broadcasted_iota(jnp.int32, (PEER_TOPK, s.shape[1]), 0)
    vals = jnp.zeros((PEER_TOPK, s.shape[1]), F32)
    rank = jnp.full(s.shape, float(PEER_TOPK), F32) if want_rank else None
    cur = s
    for it in range(PEER_TOPK):
        m = jnp.max(cur, axis=0, keepdims=True)
        vals = jnp.where(row == it, m, vals)
        eq = cur == m
        if want_rank:
            rank = jnp.where(eq, float(it), rank)
        cur = jnp.where(eq, NEG_INF, cur)
    return vals, rank, _count_rows(cur == NEG_INF)


def _top16_exact(s):
    row = lax.broadcasted_iota(jnp.int32, (PEER_TOPK, s.shape[1]), 0)
    idx = lax.broadcasted_iota(jnp.int32, s.shape, 0).astype(F32)
    vals = jnp.zeros((PEER_TOPK, s.shape[1]), F32)
    rank = jnp.full(s.shape, float(PEER_TOPK), F32)
    cur = s
    for it in range(PEER_TOPK):
        m = jnp.max(cur, axis=0, keepdims=True)
        vals = jnp.where(row == it, m, vals)
        hit = idx == jnp.min(jnp.where(cur == m, idx, float(s.shape[0])), axis=0, keepdims=True)
        rank = jnp.where(hit, float(it), rank)
        cur = jnp.where(hit, NEG_INF, cur)
    return vals, rank


def _head_fast(s1, s2, pair_ties=False):
    tt = s1.shape[1]
    row8 = lax.broadcasted_iota(jnp.int32, (SUBLANES, tt), 0)
    v1, _, gone1 = _top16_fast(s1, False)
    v2, rank2, gone2 = _top16_fast(s2, True)
    cands = [v1[0:1] + v2]
    firsts = [jnp.broadcast_to(v1[0:1], (PEER_TOPK, tt))]
    v2lo = v2[0:SUBLANES]
    for a, nb in PAIR_ROWS:
        cands.append(jnp.where(row8 < nb, v1[a:a + 1] + v2lo, NEG_INF))
        firsts.append(jnp.broadcast_to(v1[a:a + 1], (SUBLANES, tt)))
    cands.append(v1[SUBLANES:PEER_TOPK] + v2[0:1])
    firsts.append(v1[SUBLANES:PEER_TOPK])
    cur = jnp.concatenate(cands, axis=0)
    top = v1[0:1] + v2[0:1]
    zsum = jnp.zeros((1, tt), F32)
    tau = top
    cnt16 = jnp.zeros((PEER_TOPK, tt), F32)
    if pair_ties:
        first = jnp.concatenate(firsts, axis=0)
        tau1 = v1[0:1]
        for _ in range(PEER_TOPK):
            tau = jnp.max(cur, axis=0, keepdims=True)
            eq = cur == tau
            tau1 = jnp.max(jnp.where(eq, first, NEG_INF), axis=0, keepdims=True)
            zsum = zsum + jnp.exp(tau - top)
            cur = jnp.where(eq, jnp.where(first == tau1, NEG_INF, cur), cur)
        gone3 = jnp.full((1, tt), float(PEER_TOPK), F32)
        last_ok = jnp.where(v1 >= tau1, 1.0, 0.0)
        for b in range(PEER_TOPK):
            pair = v1 + v2[b:b + 1]
            cnt16 = cnt16 + jnp.where(pair > tau, 1.0, jnp.where(pair == tau, last_ok, 0.0))
    else:
        pairs = cur
        tau_prev = top
        n_last = None
        for it in range(PEER_TOPK):
            tau_prev = tau
            tau = jnp.max(cur, axis=0, keepdims=True)
            eq = cur == tau
            if it == PEER_TOPK - 1:
                n_last = _count_rows(eq)
            cur = jnp.where(eq, NEG_INF, cur)
        gone3 = _count_rows(cur == NEG_INF) - float(N_PAIR_PAD)
        one_extra = jnp.logical_and(gone3 == float(PEER_TOPK + 1), n_last == 1.0)
        tau = jnp.where(one_extra, tau_prev, tau)
        gone3 = jnp.where(one_extra, float(PEER_TOPK), gone3)
        zsum = jnp.sum(jnp.where(pairs >= tau, jnp.exp(pairs - top), 0.0), axis=0, keepdims=True)
        for b in range(PEER_TOPK):
            cnt16 = cnt16 + jnp.where(v1 + v2[b:b + 1] >= tau, 1.0, 0.0)
    cnt = jnp.zeros((N_KEYS, tt), F32)
    for a in range(PEER_TOPK):
        cnt = jnp.where(s1 == v1[a:a + 1], cnt16[a:a + 1], cnt)
    tables = (rank2, jnp.exp(s2 - v2[0:1]), cnt, jnp.exp(s1 - v1[0:1]) / zsum)
    return tables, jnp.maximum(gone1, gone2), gone3


def _head_exact(s1, s2):
    tt = s1.shape[1]
    v1, rank1 = _top16_exact(s1)
    v2, rank2 = _top16_exact(s2)
    sums = jnp.concatenate([v1[a:a + 1] + v2 for a in range(PEER_TOPK)], axis=0)
    n_pairs = PEER_TOPK * PEER_TOPK
    flat = lax.broadcasted_iota(jnp.int32, (n_pairs, tt), 0).astype(F32)
    top = v1[0:1] + v2[0:1]
    zsum = jnp.zeros((1, tt), F32)
    cur = sums
    tau = top
    last = jnp.zeros((1, tt), F32)
    for _ in range(PEER_TOPK):
        tau = jnp.max(cur, axis=0, keepdims=True)
        last = jnp.min(jnp.where(cur == tau, flat, float(n_pairs)), axis=0, keepdims=True)
        zsum = zsum + jnp.exp(tau - top)
        cur = jnp.where(flat == last, NEG_INF, cur)
    taken = jnp.where(sums > tau, 1.0, jnp.where(sums == tau, jnp.where(flat <= last, 1.0, 0.0), 0.0))
    cnt = jnp.zeros((N_KEYS, tt), F32)
    for a in range(PEER_TOPK):
        cnt_a = jnp.sum(taken[a * PEER_TOPK:(a + 1) * PEER_TOPK], axis=0, keepdims=True)
        cnt = jnp.where(rank1 == float(a), cnt_a, cnt)
    return rank2, jnp.exp(s2 - v2[0:1]), cnt, jnp.exp(s1 - v1[0:1]) / zsum


def _topk_body(h2t_ref, wqt_ref, keys_ref, rk2_ref, e2_ref, cnt_ref, c_ref, s_scr, tie_scr):
    qt = _dot(wqt_ref[...], h2t_ref[...])

    def store(h, tables):
        rank2, e2, cnt, c = tables
        rk2_ref[h] = rank2.astype(BF16)
        e2_ref[h] = e2.astype(BF16)
        cnt_ref[h] = cnt
        c_ref[h] = c

    for h in range(PEER_HEADS):
        base = h * 2 * PEER_HALF
        s1 = _dot(keys_ref[h, 0], qt[base:base + PEER_HALF].astype(BF16))
        s2 = _dot(keys_ref[h, 1], qt[base + PEER_HALF:base + 2 * PEER_HALF].astype(BF16))
        tables, gone_half, gone_pair = _head_fast(s1, s2)
        store(h, tables)
        s_scr[2 * h] = s1
        s_scr[2 * h + 1] = s2
        half_tie = (jnp.max(gone_half) > float(PEER_TOPK)).astype(jnp.int32)
        pair_tie = (jnp.max(gone_pair) > float(PEER_TOPK)).astype(jnp.int32)
        tie_scr[h] = jnp.maximum(2 * half_tie, pair_tie)

    def redo(h, carry):
        @pl.when(tie_scr[h] == 2)
        def _():
            store(h, _head_exact(s_scr[2 * h], s_scr[2 * h + 1]))

        @pl.when(tie_scr[h] == 1)
        def _():
            store(h, _head_fast(s_scr[2 * h], s_scr[2 * h + 1], pair_ties=True)[0])
        return carry

    lax.fori_loop(0, PEER_HEADS, redo, 0)


def _topk_call(h2t, wqt, keys, tt):
    n = h2t.shape[1]
    tab = pl.BlockSpec((PEER_HEADS, N_KEYS, tt), lambda i: (0, 0, i))
    tab_f32 = jax.ShapeDtypeStruct((PEER_HEADS, N_KEYS, n), F32)
    tab_b16 = jax.ShapeDtypeStruct((PEER_HEADS, N_KEYS, n), BF16)
    return pl.pallas_call(
        _topk_body,
        grid=(n // tt,),
        in_specs=[pl.BlockSpec((D_MODEL, tt), lambda i: (0, i)),
                  pl.BlockSpec(wqt.shape, lambda i: (0, 0)),
                  pl.BlockSpec(keys.shape, lambda i: (0, 0, 0, 0))],
        out_specs=[tab, tab, tab, tab],
        out_shape=[tab_b16, tab_b16, tab_f32, tab_f32],
        scratch_shapes=[pltpu.VMEM((2 * PEER_HEADS, N_KEYS, tt), F32), pltpu.SMEM((PEER_HEADS,), jnp.int32)],
        compiler_params=_cparams(("parallel",)),
        name="topk",
    )(h2t, wqt, keys)


def _gelu(x):
    x = x.astype(BF16)
    return (0.5 * x) * (1.0 + lax.erf(x * (2.0 ** -0.5)))


def _peer_gate_mul(act_scr, w_scr, rk2_ref, e2_ref, cnt_ref, c_ref, chunk, groups):
    tt = act_scr.shape[1]
    zero = jnp.zeros((), BF16)
    key0 = pl.multiple_of(chunk * groups, groups)
    for ii in range(groups):
        gate = None
        for h in range(PEER_HEADS):
            cnt_row = cnt_ref[h, pl.ds(key0, groups), :][ii:ii + 1]
            c_row = c_ref[h, pl.ds(key0, groups), :][ii:ii + 1]
            cnt_t = jnp.broadcast_to(cnt_row, (N_KEYS, tt)).astype(BF16)
            c_t = jnp.broadcast_to(c_row, (N_KEYS, tt)).astype(BF16)
            term = jnp.where(rk2_ref[h] < cnt_t, e2_ref[h], zero) * c_t
            gate = term if gate is None else gate + term
        rows = slice(ii * N_KEYS, (ii + 1) * N_KEYS)
        w_scr[rows, :] = act_scr[rows, :] * gate


def _peer_body(h2t_ref, x1_ref, rk2_ref, e2_ref, cnt_ref, c_ref, u_ref, vt_ref, nw_ref, y_ref,
               acc_scr, act_scr, w_scr, *, ec):
    e = pl.program_id(1)
    groups = ec // N_KEYS

    @pl.when(e == 0)
    def _():
        acc_scr[...] = jnp.zeros_like(acc_scr)

    act_scr[...] = _gelu(_dot(u_ref[...], h2t_ref[...]))
    _peer_gate_mul(act_scr, w_scr, rk2_ref, e2_ref, cnt_ref, c_ref, e, groups)
    acc_scr[...] += _dot(vt_ref[...], w_scr[...])

    @pl.when(e == pl.num_programs(1) - 1)
    def _():
        x2 = x1_ref[...] + acc_scr[...].T
        ms = jnp.mean(x2 * x2, axis=-1, keepdims=True)
        y_ref[...] = x2 * lax.rsqrt(ms + EPS) * nw_ref[...]


def _peer_call(h2t, x1, rk2, e2, cnt, cc, u_b, vt_b, norm_w, tt, ec):
    n = x1.shape[0]
    assert (ec // N_KEYS) % SUBLANES == 0, "an expert chunk covers whole sublane tiles of first-half keys"
    tab = pl.BlockSpec((PEER_HEADS, N_KEYS, tt), lambda i, e: (0, 0, i))
    chunk_shape = pltpu.VMEM((ec, tt), BF16)
    return pl.pallas_call(
        functools.partial(_peer_body, ec=ec),
        grid=(n // tt, u_b.shape[0] // ec),
        in_specs=[pl.BlockSpec((D_MODEL, tt), lambda i, e: (0, i)),
                  pl.BlockSpec((tt, D_MODEL), lambda i, e: (i, 0)),
                  tab, tab, tab, tab,
                  pl.BlockSpec((ec, D_MODEL), lambda i, e: (e, 0)),
                  pl.BlockSpec((D_MODEL, ec), lambda i, e: (0, e)),
                  pl.BlockSpec((1, D_MODEL), lambda i, e: (0, 0))],
        out_specs=pl.BlockSpec((tt, D_MODEL), lambda i, e: (i, 0)),
        out_shape=jax.ShapeDtypeStruct((n, D_MODEL), F32),
        scratch_shapes=[pltpu.VMEM((D_MODEL, tt), F32), chunk_shape, chunk_shape],
        compiler_params=_cparams(("parallel", "arbitrary")),
        name="peer",
    )(h2t, x1, rk2, e2, cnt, cc, u_b, vt_b, norm_w)


def _permute_w_in(w):
    splits = [sum(IN_SIZES[:i + 1]) for i in range(len(IN_SIZES) - 1)]
    z, xbc, dt, q, k, v, glr, r, gs, gg = jnp.split(w, splits, axis=1)
    pad = jnp.zeros((D_MODEL, PROJ_COLS - COL_DTG - SSD_HEADS - GLA_GATE_RANK), w.dtype)
    return jnp.concatenate([z, v, r, gs, gg, q, k, xbc, dt, glr, pad], axis=1).astype(BF16)


def _row(v, width=None):
    v = v.reshape(1, -1).astype(F32)
    if width is not None and v.shape[1] < width:
        v = jnp.pad(v, ((0, 0), (0, width - v.shape[1])))
    return v


def _trunk(x, conv0, ssd0, gla0, wts, bg_ssd, q_ssd, bg_gla, q_gla, tm, tt, ec, proj_dtype):
    nb, length, _ = x.shape
    n = nb * length
    x2d = x.reshape(n, D_MODEL)
    proj, dtg = _proj_call(x2d, wts["norm_mix"], wts["w_in"], tm, PROJ_COLS // 3, proj_dtype)
    p3 = proj.reshape(nb, length, PROJ_COLS)
    dtg3 = dtg.reshape(nb, length, LANES)
    ys, conv_new, ssd_new = _ssd_call(p3, dtg3, conv0, ssd0, wts["conv_w"], wts["conv_b"], wts["dtb"], wts["alog"],
                                      wts["dsk"], wts["ssd_norm"], wts["eexp"], wts["eexpt"], bg_ssd, q_ssd)
    o, gla_new = _gla_call(p3, dtg3, gla0, wts["w2"], wts["b2"], wts["gla_norm"], bg_gla, q_gla)
    x1, h2t = _merge_call(x2d, ys.reshape(n, SSD_INNER), o.reshape(n, GLA_VAL), proj,
                          wts["wso"], wts["wgo"], wts["wo"], wts["norm_ffn"], tm // 2)
    rk2, e2, cnt, cc = _topk_call(h2t, wts["wqt"], wts["keys"], tt)
    y = _peer_call(h2t, x1, rk2, e2, cnt, cc, wts["u"], wts["vt"], wts["final_norm"], tt, ec)
    return y.reshape(nb, length, D_MODEL), conv_new[None], ssd_new[None], gla_new[None]


def _weights(p):
    assert p["norm_mix_w"].shape[0] == 1, "one layer"
    w2_pad = jnp.zeros((LANES, GLA_KEY), F32).at[SSD_HEADS:SSD_HEADS + GLA_GATE_RANK].set(p["w_gla_gate2"][0])
    head_of_lane = jnp.arange(SSD_INNER) // SSD_HEADDIM
    return {
        "norm_mix": _row(p["norm_mix_w"][0]),
        "w_in": _permute_w_in(p["w_in"][0]),
        "conv_w": p["conv_w"][0], "conv_b": _row(p["conv_b"][0]),
        "dtb": _row(p["dt_bias"][0], LANES), "alog": _row(p["a_log"][0], LANES),
        "dsk": _row(jnp.repeat(p["d_skip"][0], SSD_HEADDIM)),
        "ssd_norm": _row(p["ssd_norm_w"][0]),
        "eexp": (jnp.arange(LANES)[:, None] == head_of_lane[None, :]).astype(BF16),
        "eexpt": (head_of_lane[:, None] == jnp.arange(LANES)[None, :]).astype(BF16),
        "w2": w2_pad.astype(BF16), "b2": _row(p["b_gla_gate"][0]), "gla_norm": _row(p["gla_norm_w"][0]),
        "wso": p["w_ssd_out"][0].astype(BF16), "wgo": p["w_gla_out"][0].astype(BF16),
        "wo": p["w_out"][0].astype(BF16),
        "norm_ffn": _row(p["norm_ffn_w"][0]),
        "wqt": p["w_query"][0].T.astype(BF16), "keys": p["sub_keys"][0].astype(BF16),
        "u": p["expert_u"][0].astype(BF16), "vt": p["expert_v"][0].T.astype(BF16),
        "final_norm": _row(p["final_norm_w"]),
    }


def kernel(x_prompt, x_sample, state_conv, state_ssd, state_gla, norm_mix_w, w_in, conv_w, conv_b, dt_bias,
           a_log, d_skip, ssd_norm_w, w_gla_gate2, b_gla_gate, gla_norm_w, w_ssd_out, w_gla_out, w_out,
           norm_ffn_w, w_query, sub_keys, expert_u, expert_v, final_norm_w):
    wts = _weights(dict(
        norm_mix_w=norm_mix_w, w_in=w_in, conv_w=conv_w, conv_b=conv_b, dt_bias=dt_bias, a_log=a_log,
        d_skip=d_skip, ssd_norm_w=ssd_norm_w, w_gla_gate2=w_gla_gate2, b_gla_gate=b_gla_gate,
        gla_norm_w=gla_norm_w, w_ssd_out=w_ssd_out, w_gla_out=w_gla_out, w_out=w_out, norm_ffn_w=norm_ffn_w,
        w_query=w_query, sub_keys=sub_keys, expert_u=expert_u, expert_v=expert_v, final_norm_w=final_norm_w))
    bp = x_prompt.shape[0]
    conv0 = jnp.zeros((bp, CONV_W - 1, CONV_DIM), F32)
    ssd0 = jnp.zeros((bp, SSD_HEADS, SSD_HEADDIM, SSD_STATE), F32)
    gla0 = jnp.zeros((bp, GLA_HEADS, GLA_DK, GLA_DV), F32)
    ls = x_sample.shape[1]
    y_p, conv_p, ssd_p, gla_p = _trunk(x_prompt, conv0, ssd0, gla0, wts,
                                       bg_ssd=1, q_ssd=ROWS, bg_gla=2, q_gla=ROWS // 2, tm=1024, tt=512, ec=2048,
                                       proj_dtype=BF16)
    y_s, conv_s, ssd_s, gla_s = _trunk(x_sample, state_conv[0], state_ssd[0], state_gla[0], wts,
                                       bg_ssd=ROWS // ls, q_ssd=ls, bg_gla=ROWS // ls, q_gla=ls,
                                       tm=1024, tt=512, ec=2048, proj_dtype=F32)
    return (y_p, y_s, conv_p, ssd_p, gla_p, conv_s, ssd_s, gla_s)
```
